```python
import math
import jax, jax.numpy as jnp
from jax import lax
import numpy as np

D_MODEL = 1024
BATCH = 16
SEQ = 2048
DEPTH = 4

SB_HEADS = 4
SB_HD = 128
SB_W = SB_HEADS * SB_HD
SB_BLOCK = 128
GDN_HEADS = 4
GDN_HD = 128
GDN_W = GDN_HEADS * GDN_HD
GDN_CONV = 4
GDN_CHUNK = 64
GLA_HEADS = 4
GLA_KD = 128
GLA_VD = 256
GLA_KW = GLA_HEADS * GLA_KD
GLA_VW = GLA_HEADS * GLA_VD
GLA_RANK = 16
GLA_TAU = 16.0
GLA_CHUNK = 64
N_BRANCH = 3
RMS_EPS = 1e-6
L2_EPS = 1e-6

IN_SPLITS = (SB_W, SB_W, SB_W, SB_W,
             GDN_W, GDN_W, GDN_W, GDN_W, GDN_HEADS, GDN_HEADS,
             GLA_KW, GLA_KW, GLA_VW, GLA_VW, GLA_RANK,
             N_BRANCH * D_MODEL)
N_IN = sum(IN_SPLITS)

kernel_name = "hybrid_sb_gdn_gla_adaln"


def rms_norm(x, g):
    xf = x.astype(jnp.float32)
    y = xf * lax.rsqrt(jnp.mean(xf * xf, axis=-1, keepdims=True) + RMS_EPS)
    return (y * g.astype(jnp.float32)).astype(x.dtype)


def l2_norm(x):
    xf = x.astype(jnp.float32)
    return xf * lax.rsqrt(jnp.sum(xf * xf, axis=-1, keepdims=True) + L2_EPS)


def split_heads(t, n_heads):
    b, s, w = t.shape
    return t.reshape(b, s, n_heads, w // n_heads).transpose(0, 2, 1, 3)


def merge_heads(t):
    b, h, s, d = t.shape
    return t.transpose(0, 2, 1, 3).reshape(b, s, h * d)


def causal_depthwise_conv(x, w):
    kw = w.shape[0]
    return lax.conv_general_dilated(
        x, w[:, None, :].astype(x.dtype), window_strides=(1,), padding=[(kw - 1, 0)],
        dimension_numbers=('NWC', 'WIO', 'NWC'), feature_group_count=x.shape[-1])


def stick_breaking_attention(q, k, v):
    b, h, s, d = q.shape
    nb = s // SB_BLOCK
    scale = d ** -0.5
    key_pos = jnp.arange(s)
    qb = q.reshape(b, h, nb, SB_BLOCK, d).transpose(2, 0, 1, 3, 4)

    def block(args):
        q_blk, i = args
        z = jnp.einsum('bhqd,bhkd->bhqk', q_blk, k) * scale
        q_pos = i * SB_BLOCK + jnp.arange(SB_BLOCK)
        mask = key_pos[None, :] < q_pos[:, None]
        log_1m = jnp.where(mask, jax.nn.log_sigmoid(-z), 0.0)
        tail = lax.cumsum(log_1m, axis=3, reverse=True) - log_1m
        w = jnp.where(mask, jnp.exp(jax.nn.log_sigmoid(z) + tail), 0.0)
        return jnp.einsum('bhqk,bhkd->bhqd', w, v)

    out = lax.map(block, (qb, jnp.arange(nb)))
    return out.transpose(1, 2, 0, 3, 4).reshape(b, h, s, d)


def gated_delta_rule(q, k, v, beta, g):
    b, h, s, dk = q.shape
    dv = v.shape[-1]
    c = GDN_CHUNK
    n = s // c
    q, k, v = (t.reshape(b, h, n, c, t.shape[-1]) for t in (q, k, v))
    beta = beta.reshape(b, h, n, c)
    gam = jnp.cumsum(g.reshape(b, h, n, c), axis=-1)
    tri_incl = jnp.tril(jnp.ones((c, c), bool))
    tri_strict = jnp.tril(jnp.ones((c, c), bool), -1)
    diff = gam[..., :, None] - gam[..., None, :]
    decay = jnp.where(tri_incl, jnp.exp(jnp.where(tri_incl, diff, 0.0)), 0.0)
    a_mat = jnp.where(tri_strict,
                      beta[..., None] * jnp.einsum('bhnid,bhnjd->bhnij', k, k) * decay, 0.0)
    rhs = jnp.concatenate([v * beta[..., None], k * (beta * jnp.exp(gam))[..., None]], axis=-1)
    sol = lax.linalg.triangular_solve(a_mat, rhs, left_side=True, lower=True, unit_diagonal=True)
    u_base, w_kc = sol[..., :dv], sol[..., dv:]
    qk = jnp.einsum('bhnid,bhnjd->bhnij', q, k) * decay
    q_dec = q * jnp.exp(gam)[..., None]
    k_dec = k * jnp.exp(gam[..., -1:] - gam)[..., None]
    chunk_decay = jnp.exp(gam[..., -1])
    xs = tuple(jnp.moveaxis(t, 2, 0) for t in (u_base, w_kc, qk, q_dec, k_dec, chunk_decay))

    def step(state, inp):
        u_b, w_c, qk_c, qd, kd, cd = inp
        u = u_b - jnp.einsum('bhck,bhkv->bhcv', w_c, state)
        o = jnp.einsum('bhck,bhkv->bhcv', qd, state) + jnp.einsum('bhij,bhjv->bhiv', qk_c, u)
        state = state * cd[..., None, None] + jnp.einsum('bhck,bhcv->bhkv', kd, u)
        return state, o

    s0 = jnp.zeros((b, h, dk, dv), jnp.float32)
    _, o = lax.scan(step, s0, xs)
    return jnp.moveaxis(o, 0, 2).reshape(b, h, s, dv)


def gla_chunked(q, k, v, log_a):
    b, h, s, dk = q.shape
    dv = v.shape[-1]
    c = GLA_CHUNK
    n = s // c
    to_chunks = lambda t: jnp.moveaxis(t.reshape(b, h, n, c, t.shape[-1]), 2, 0)
    gam = jnp.cumsum(log_a.reshape(b, h, n, c, dk), axis=3)
    xs = (to_chunks(q), to_chunks(k), to_chunks(v), jnp.moveaxis(gam, 2, 0))
    tri = jnp.tril(jnp.ones((c, c), bool))

    def step(state, inp):
        qc, kc, vc, gc = inp
        diff = gc[:, :, :, None, :] - gc[:, :, None, :, :]
        pair = jnp.exp(jnp.where(tri[:, :, None], diff, -jnp.inf))
        att = jnp.einsum('bhid,bhjd,bhijd->bhij', qc, kc, pair)
        o = (jnp.einsum('bhik,bhkv->bhiv', qc * jnp.exp(gc), state)
             + jnp.einsum('bhij,bhjv->bhiv', att, vc))
        g_last = gc[:, :, -1:, :]
        state = (state * jnp.exp(g_last[:, :, 0, :, None])
                 + jnp.einsum('bhjk,bhjv->bhkv', kc * jnp.exp(g_last - gc), vc))
        return state, o

    s0 = jnp.zeros((b, h, dk, dv), jnp.float32)
    _, o = lax.scan(step, s0, xs)
    return jnp.moveaxis(o, 0, 2).reshape(b, h, s, dv)


def hybrid_layer(x, c_act, ada_w, ada_b, norm_g, w_in, sb_qnorm, sb_knorm, gdn_conv, gdn_a_log,
                 gdn_dt_bias, gdn_onorm, gla_w2, gla_b, gla_onorm, merge_b, proj_sb, proj_gdn,
                 proj_gla, w_out):
    f32 = jnp.float32
    shift, scale, gate = jnp.split(c_act @ ada_w + ada_b, 3, axis=-1)
    h = rms_norm(x, norm_g) * (1 + scale[:, None, :]) + shift[:, None, :]
    offsets = np.cumsum(IN_SPLITS)[:-1].tolist()
    (sq, sk, sv, sz, dq, dk_, dv_, dz, db, da,
     lq, lk, lv, lz, lr, mg) = jnp.split(h @ w_in, offsets, axis=-1)

    q = rms_norm(split_heads(sq, SB_HEADS), sb_qnorm).astype(f32)
    k = rms_norm(split_heads(sk, SB_HEADS), sb_knorm).astype(f32)
    o_sb = stick_breaking_attention(q, k, split_heads(sv, SB_HEADS).astype(f32))
    y_sb = merge_heads(o_sb).astype(x.dtype) * jax.nn.silu(sz)

    qkv = jax.nn.silu(causal_depthwise_conv(jnp.concatenate([dq, dk_, dv_], axis=-1), gdn_conv))
    gq, gk, gv = jnp.split(qkv, 3, axis=-1)
    gq = l2_norm(split_heads(gq, GDN_HEADS)) * (GDN_HD ** -0.5)
    gk = l2_norm(split_heads(gk, GDN_HEADS))
    beta = jax.nn.sigmoid(db.astype(f32)).transpose(0, 2, 1)
    g = (-jnp.exp(gdn_a_log.astype(f32))
         * jax.nn.softplus(da.astype(f32) + gdn_dt_bias.astype(f32))).transpose(0, 2, 1)
    o_gdn = gated_delta_rule(gq, gk, split_heads(gv, GDN_HEADS).astype(f32), beta, g)
    y_gdn = merge_heads(rms_norm(o_gdn, gdn_onorm)).astype(x.dtype) * jax.nn.silu(dz)

    log_a = jax.nn.log_sigmoid((lr @ gla_w2 + gla_b).astype(f32)) / GLA_TAU
    o_gla = gla_chunked(split_heads(lq, GLA_HEADS).astype(f32) * (GLA_KD ** -0.5),
                        split_heads(lk, GLA_HEADS).astype(f32),
                        split_heads(lv, GLA_HEADS).astype(f32),
                        split_heads(log_a, GLA_HEADS))
    y_gla = merge_heads(rms_norm(o_gla, gla_onorm)).astype(x.dtype) * jax.nn.silu(lz)

    g_sb, g_gdn, g_gla = jnp.split(jax.nn.sigmoid(mg + merge_b), 3, axis=-1)
    merged = g_sb * (y_sb @ proj_sb) + g_gdn * (y_gdn @ proj_gdn) + g_gla * (y_gla @ proj_gla)
    return x + gate[:, None, :] * (merged @ w_out)


def setup_inputs(seed: int = 0) -> dict:
    key = jax.random.key(seed)
    ks = jax.random.split(key, 24)
    L, D = DEPTH, D_MODEL
    nrm = lambda k, shape, s: jax.random.normal(k, shape, jnp.float32) * s
    dt = jnp.exp(jax.random.uniform(ks[10], (L, GDN_HEADS), jnp.float32,
                                    minval=math.log(1e-3), maxval=math.log(1e-1)))
    return {
        "x": nrm(ks[0], (BATCH, SEQ, D), 1.0),
        "c": nrm(ks[1], (BATCH, D), 1.0),
        "ada_w": nrm(ks[2], (L, D, 3 * D), 0.5 * D ** -0.5),
        "ada_b": nrm(ks[3], (L, 3 * D), 0.02),
        "norm_g": 1.0 + nrm(ks[4], (L, D), 0.02),
        "w_in": nrm(ks[5], (L, D, N_IN), D ** -0.5),
        "sb_qnorm": 1.0 + nrm(ks[6], (L, SB_HD), 0.02),
        "sb_knorm": 1.0 + nrm(ks[7], (L, SB_HD), 0.02),
        "gdn_conv": nrm(ks[8], (L, GDN_CONV, 3 * GDN_W), GDN_CONV ** -0.5),
        "gdn_a_log": jnp.log(jax.random.uniform(ks[9], (L, GDN_HEADS), jnp.float32,
                                                minval=1.0, maxval=16.0)),
        "gdn_dt_bias": dt + jnp.log(-jnp.expm1(-dt)),
        "gdn_onorm": 1.0 + nrm(ks[11], (L, GDN_HD), 0.02),
        "gla_w2": nrm(ks[12], (L, GLA_RANK, GLA_KW), GLA_RANK ** -0.5),
        "gla_b": nrm(ks[13], (L, GLA_KW), 0.1),
        "gla_onorm": 1.0 + nrm(ks[14], (L, GLA_VD), 0.02),
        "merge_b": nrm(ks[15], (L, N_BRANCH * D), 0.1),
        "proj_sb": nrm(ks[16], (L, SB_W, D), SB_W ** -0.5),
        "proj_gdn": nrm(ks[17], (L, GDN_W, D), GDN_W ** -0.5),
        "proj_gla": nrm(ks[18], (L, GLA_VW, D), GLA_VW ** -0.5),
        "w_out": nrm(ks[19], (L, D, D), D ** -0.5),
    }


def reference(x, c, ada_w, ada_b, norm_g, w_in, sb_qnorm, sb_knorm, gdn_conv, gdn_a_log,
              gdn_dt_bias, gdn_onorm, gla_w2, gla_b, gla_onorm, merge_b, proj_sb, proj_gdn,
              proj_gla, w_out):
    c_act = jax.nn.silu(c)
    for l in range(DEPTH):
        x = hybrid_layer(x, c_act, ada_w[l], ada_b[l], norm_g[l], w_in[l], sb_qnorm[l],
                         sb_knorm[l], gdn_conv[l], gdn_a_log[l], gdn_dt_bias[l], gdn_onorm[l],
                         gla_w2[l], gla_b[l], gla_onorm[l], merge_b[l], proj_sb[l], proj_gdn[l],
                         proj_gla[l], w_out[l])
    return x
```

```python
import functools

import jax
import jax.numpy as jnp
from jax import lax
from jax.experimental import pallas as pl
from jax.experimental.pallas import tpu as pltpu

F32 = jnp.float32
BF16 = jnp.bfloat16

D_MODEL = 1024
N_HEADS = 4
HEAD_DIM = 128
GLA_VD = 256
GDN_CONV = 4
GLA_RANK = 16
GLA_TAU = 16.0
N_BRANCH = 3
RMS_EPS = 1e-6
L2_EPS = 1e-6
CHUNK = 64
SUB = 16
LANES = 128
VMEM_LIMIT = 56 * 1024 * 1024

OFF_MG = 0
OFF_DQKV = 3072
OFF_DZ = 4608
OFF_SQ, OFF_SK, OFF_SV, OFF_SZ = 5120, 5632, 6144, 6656
OFF_LQ, OFF_LK, OFF_LV, OFF_LZ = 7168, 7680, 8192, 9216
N_MAIN = 10240
SM_LR, SM_DB, SM_DA = 0, 16, 20


def _dot(a, b):
    return jnp.dot(a, b, preferred_element_type=F32)


def _dot_nt(a, b):
    return lax.dot_general(a, b, (((1,), (1,)), ((), ())), preferred_element_type=F32)


def _dot_tn(a, b):
    return lax.dot_general(a, b, (((0,), (0,)), ((), ())), preferred_element_type=F32)


def _split(x):
    hi = x.astype(BF16)
    lo = (x - hi.astype(F32)).astype(BF16)
    return hi, lo


def _sigmoid(x):
    return 1.0 / (1.0 + jnp.exp(-x))


def _softplus(x):
    return jnp.maximum(x, 0.0) + jnp.log(1.0 + jnp.exp(-jnp.abs(x)))


def _iota(shape, axis):
    return lax.broadcasted_iota(jnp.int32, shape, axis)


def _ada_kernel(c_ref, w_ref, b_ref, o_ref):
    c = c_ref[...]
    ca = c * _sigmoid(c)
    o_ref[...] = jnp.dot(ca, w_ref[...], preferred_element_type=F32,
                         precision=lax.Precision.HIGHEST) + b_ref[...]


def _ada_call(c, ada_w, ada_b):
    n_l, d, d3 = ada_w.shape
    b = c.shape[0]
    tn = 768
    return pl.pallas_call(
        _ada_kernel,
        out_shape=jax.ShapeDtypeStruct((n_l, b, d3), F32),
        grid=(n_l, d3 // tn),
        in_specs=[
            pl.BlockSpec((b, d), lambda l, j: (0, 0)),
            pl.BlockSpec((None, d, tn), lambda l, j: (l, 0, j)),
            pl.BlockSpec((None, 1, tn), lambda l, j: (l, 0, j)),
        ],
        out_specs=pl.BlockSpec((None, b, tn), lambda l, j: (l, 0, j)),
        compiler_params=pltpu.CompilerParams(
            dimension_semantics=("arbitrary", "arbitrary"), vmem_limit_bytes=VMEM_LIMIT),
        name="adaln_mod",
    )(c, ada_w, ada_b.reshape(n_l, 1, d3))


def _in_kernel(x_ref, mod_ref, g_ref, w_ref, ws_ref, out_ref, small_ref, h_ref, *, slab):
    tm, d = x_ref.shape

    @pl.when(pl.program_id(1) == 0)
    def _():
        shift = mod_ref[:, 0:d]
        scale1 = 1.0 + mod_ref[:, d:2 * d]
        gain = g_ref[...]

        def body(s, carry):
            rows = pl.ds(pl.multiple_of(s * slab, slab), slab)
            x = x_ref[rows, :]
            y = x * lax.rsqrt(jnp.mean(x * x, axis=-1, keepdims=True) + RMS_EPS) * gain
            h_ref[rows, :] = (y * scale1 + shift).astype(BF16)
            return carry

        lax.fori_loop(0, tm // slab, body, 0)
        small_ref[...] = _dot(h_ref[...], ws_ref[...])

    out_ref[...] = _dot(h_ref[...], w_ref[...]).astype(BF16)


def _in_call(x2, mod, norm_g, w_main, w_small, layer, seq, *, tm, tn):
    t, d = x2.shape
    n_main = w_main.shape[-1]
    tm = min(tm, seq)
    rows_per_seq = seq // tm
    kern = functools.partial(_in_kernel, slab=min(128, tm))
    return pl.pallas_call(
        kern,
        out_shape=(jax.ShapeDtypeStruct((t, n_main), BF16),
                   jax.ShapeDtypeStruct((t, LANES), F32)),
        grid=(t // tm, n_main // tn),
        in_specs=[
            pl.BlockSpec((tm, d), lambda i, j: (i, 0)),
            pl.BlockSpec((None, None, 1, 3 * d), lambda i, j: (layer, i // rows_per_seq, 0, 0)),
            pl.BlockSpec((None, 1, d), lambda i, j: (layer, 0, 0)),
            pl.BlockSpec((None, d, tn), lambda i, j: (layer, 0, j)),
            pl.BlockSpec((None, d, LANES), lambda i, j: (layer, 0, 0)),
        ],
        out_specs=(pl.BlockSpec((tm, tn), lambda i, j: (i, j)),
                   pl.BlockSpec((tm, LANES), lambda i, j: (i, 0))),
        scratch_shapes=[pltpu.VMEM((tm, d), BF16)],
        compiler_params=pltpu.CompilerParams(
            dimension_semantics=("arbitrary", "arbitrary"), vmem_limit_bytes=VMEM_LIMIT),
        name="in_proj",
    )(x2, mod, norm_g, w_main, w_small)


def _sb_kernel(q_ref, k_ref, v_ref, z_ref, qw_ref, kw_ref, o_ref, kn_ref, *, tq, tk):
    seq, hd = q_ref.shape
    scale = hd ** -0.5

    def rms(x, w):
        return x * lax.rsqrt(jnp.mean(x * x, axis=-1, keepdims=True) + RMS_EPS) * w

    def knorm(s, carry):
        rows = pl.ds(pl.multiple_of(s * tq, tq), tq)
        kn_ref[rows, :] = rms(k_ref[rows, :].astype(F32), kw_ref[...]).astype(BF16)
        return carry

    lax.fori_loop(0, seq // tq, knorm, 0)

    jj = _iota((2 * tk, 2 * tk), 0) % tk
    ss = _iota((2 * tk, 2 * tk), 1)
    tail_w = jnp.where((ss >= tk) | (jj > ss), 1.0, 0.0).astype(BF16)

    def key_block(kb, carry, qn, qpos, masked):
        acc, run = carry
        c0 = pl.multiple_of(kb * tk, tk)
        kn = kn_ref[pl.ds(c0, tk), :]
        vb = v_ref[pl.ds(c0, tk), :]
        z = _dot_nt(qn, kn)
        lg = -_softplus(z)
        if masked:
            mask = (c0 + _iota((tq, tk), 1)) < qpos
            lg = jnp.where(mask, lg, 0.0)
        hi, lo = _split(lg)
        sums = _dot(jnp.concatenate([hi, lo], axis=1), tail_w)
        w = jnp.exp(lg + z + sums[:, :tk] + run)
        if masked:
            w = jnp.where(mask, w, 0.0)
        acc = acc + _dot(w.astype(BF16), vb)
        run = run + sums[:, tk:]
        return acc, run

    n_sub = tq // tk

    def q_block(i, carry):
        r0 = pl.multiple_of(i * tq, tq)
        rows = pl.ds(r0, tq)
        qn = (rms(q_ref[rows, :].astype(F32), qw_ref[...]) * scale).astype(BF16)
        qpos = r0 + _iota((tq, tk), 0)
        state = (jnp.zeros((tq, hd), F32), jnp.zeros((tq, tk), F32))
        for d in range(n_sub):
            state = key_block(i * n_sub + (n_sub - 1 - d), state, qn, qpos, True)
        state = lax.fori_loop(
            0, i * n_sub,
            lambda n, st: key_block(i * n_sub - 1 - n, st, qn, qpos, False), state)
        zg = z_ref[rows, :].astype(F32)
        o_ref[rows, :] = (state[0] * (zg * _sigmoid(zg))).astype(BF16)
        return carry

    lax.fori_loop(0, seq // tq, q_block, 0)


def _sb_call(proj3, qw, kw, layer, *, tq=256, tk=128):
    b, seq, _ = proj3.shape
    tq = min(tq, seq)
    hd = HEAD_DIM

    def col(off):
        return pl.BlockSpec((None, seq, hd), lambda bi, h: (bi, 0, off // hd + h))

    wspec = pl.BlockSpec((None, 1, hd), lambda bi, h: (layer, 0, 0))
    return pl.pallas_call(
        functools.partial(_sb_kernel, tq=tq, tk=tk),
        out_shape=jax.ShapeDtypeStruct((b, seq, N_HEADS * hd), BF16),
        grid=(b, N_HEADS),
        in_specs=[col(OFF_SQ), col(OFF_SK), col(OFF_SV), col(OFF_SZ), wspec, wspec],
        out_specs=pl.BlockSpec((None, seq, hd), lambda bi, h: (bi, 0, h)),
        scratch_shapes=[pltpu.VMEM((seq, hd), BF16)],
        compiler_params=pltpu.CompilerParams(
            dimension_semantics=("arbitrary", "arbitrary"), vmem_limit_bytes=VMEM_LIMIT),
        name="sb_attn",
    )(proj3, proj3, proj3, proj3, qw, kw)


def _unit_lower_inverse(a, eye):
    t = eye - a
    p = a
    for _ in range(5):
        pb = p.astype(BF16)
        p = _dot(pb, pb)
        t = t + _dot(t.astype(BF16), p.astype(BF16))
    return t


def _gdn_kernel(qkv_ref, prev_ref, dz_ref, sm_ref, conv_ref, hp_ref, on_ref, o_ref,
                xs_ref, st_ref):
    rblk = qkv_ref.shape[0]
    hd = HEAD_DIM
    c = CHUNK
    halo = prev_ref.shape[0]
    blk = pl.program_id(1)

    @pl.when(blk == 0)
    def _():
        st_ref[...] = jnp.zeros_like(st_ref)
        xs_ref[0:halo, :] = jnp.zeros((halo, xs_ref.shape[1]), F32)

    @pl.when(blk > 0)
    def _():
        xs_ref[0:halo, :] = prev_ref[...].astype(F32)

    def stage(s, carry):
        rows = pl.multiple_of(s * c, c)
        xs_ref[pl.ds(halo + rows, c), :] = qkv_ref[pl.ds(rows, c), :].astype(F32)
        return carry

    lax.fori_loop(0, rblk // c, stage, 0)

    ri = _iota((c, c), 0)
    ci = _iota((c, c), 1)
    tri_incl = ri >= ci
    tri_strict = ri > ci
    eye = jnp.where(ri == ci, 1.0, 0.0).astype(F32)
    tri2 = jnp.where(_iota((c, 2 * c), 0) >= (_iota((c, 2 * c), 1) % c), 1.0, 0.0).astype(BF16)
    later = jnp.where(_iota((c, LANES), 0) > _iota((c, LANES), 1), 1.0, 0.0).astype(F32)

    def chunk(n, carry):
        r0 = pl.multiple_of(n * c, c)
        rows = pl.ds(r0, c)
        win = pl.ds(r0, c + halo)
        sm = sm_ref[rows, :]
        for h in range(N_HEADS):
            hc = slice(h * hd, (h + 1) * hd)

            def conv(col0):
                w = xs_ref[win, col0:col0 + hd]
                cw = conv_ref[:, col0:col0 + hd]
                y = w[halo:] * cw[GDN_CONV - 1:GDN_CONV]
                for j in range(1, GDN_CONV):
                    y = y + pltpu.roll(w, j, axis=0)[halo:] * cw[GDN_CONV - 1 - j:GDN_CONV - j]
                return y * _sigmoid(y)

            q = conv(h * hd)
            k = conv(N_HEADS * hd + h * hd)
            v = conv(2 * N_HEADS * hd + h * hd)
            q = q * (lax.rsqrt(jnp.sum(q * q, axis=-1, keepdims=True) + L2_EPS) * (hd ** -0.5))
            k = k * lax.rsqrt(jnp.sum(k * k, axis=-1, keepdims=True) + L2_EPS)
            beta = _sigmoid(jnp.broadcast_to(sm[:, SM_DB + h:SM_DB + h + 1], (c, hd)))
            da = jnp.broadcast_to(sm[:, SM_DA + h:SM_DA + h + 1], (c, hd))
            g = -jnp.exp(hp_ref[0:1, hc]) * _softplus(da + hp_ref[1:2, hc])

            xg = jnp.concatenate([g, g * later], axis=1)
            xh, xl = _split(xg)
            cum = _dot(tri2, jnp.concatenate([xh, xl], axis=0))
            gam = cum[:, :hd]
            diff = cum[:, hd:hd + c]
            decay = jnp.where(tri_incl, jnp.exp(diff), 0.0)
            eg = jnp.exp(gam)
            gl = gam[c - 1:c, :]

            kb = k.astype(BF16)
            kk = _dot_nt(kb, kb)
            qk = _dot_nt(q.astype(BF16), kb) * decay
            a = jnp.where(tri_strict, beta[:, :c] * kk * decay, 0.0)
            tinv = _unit_lower_inverse(a, eye)
            rhs = jnp.concatenate([v * beta, k * (beta * eg)], axis=1).astype(BF16)
            sol = _dot(tinv.astype(BF16), rhs)

            s_old = st_ref[h]
            lhs = jnp.concatenate([sol[:, hd:], q * eg], axis=0).astype(BF16)
            r = _dot(lhs, s_old.astype(BF16))
            u = sol[:, :hd] - r[:c]
            ub = u.astype(BF16)
            o = r[c:] + _dot(qk.astype(BF16), ub)
            kd = (k * jnp.exp(gl - gam)).astype(BF16)
            st_ref[h] = s_old * jnp.exp(gl) + _dot_tn(kd, ub)

            on = o * lax.rsqrt(jnp.mean(o * o, axis=-1, keepdims=True) + RMS_EPS) * on_ref[...]
            zg = dz_ref[rows, hc].astype(F32)
            o_ref[rows, hc] = (on * (zg * _sigmoid(zg))).astype(BF16)
        return carry

    lax.fori_loop(0, rblk // c, chunk, 0)


def _gdn_call(proj3, small3, conv_w, hp, onorm, layer, *, rblk=512, halo=16):
    b, seq, _ = proj3.shape
    rblk = min(rblk, seq)
    w3 = 3 * N_HEADS * HEAD_DIM
    wz = N_HEADS * HEAD_DIM
    per = rblk // halo
    return pl.pallas_call(
        _gdn_kernel,
        out_shape=jax.ShapeDtypeStruct((b, seq, wz), BF16),
        grid=(b, seq // rblk),
        in_specs=[
            pl.BlockSpec((None, rblk, w3), lambda bi, i: (bi, i, OFF_DQKV // w3)),
            pl.BlockSpec((None, halo, w3),
                         lambda bi, i: (bi, jnp.maximum(i * per - 1, 0), OFF_DQKV // w3)),
            pl.BlockSpec((None, rblk, wz), lambda bi, i: (bi, i, OFF_DZ // wz)),
            pl.BlockSpec((None, rblk, LANES), lambda bi, i: (bi, i, 0)),
            pl.BlockSpec((None, GDN_CONV, w3), lambda bi, i: (layer, 0, 0)),
            pl.BlockSpec((None, 2, wz), lambda bi, i: (layer, 0, 0)),
            pl.BlockSpec((None, 1, HEAD_DIM), lambda bi, i: (layer, 0, 0)),
        ],
        out_specs=pl.BlockSpec((None, rblk, wz), lambda bi, i: (bi, i, 0)),
        scratch_shapes=[pltpu.VMEM((rblk + halo, w3), F32),
                        pltpu.VMEM((N_HEADS, HEAD_DIM, HEAD_DIM), F32)],
        compiler_params=pltpu.CompilerParams(
            dimension_semantics=("arbitrary", "arbitrary"), vmem_limit_bytes=VMEM_LIMIT),
        name="gdn",
    )(proj3, proj3, proj3, small3, conv_w, hp, onorm)


def _gla_kernel(q_ref, k_ref, v_ref, z_ref, sm_ref, w2_ref, b_ref, on_ref, o_ref, la_ref, st_ref):
    rblk = q_ref.shape[0]
    hd = HEAD_DIM
    vd = GLA_VD
    c = CHUNK

    @pl.when(pl.program_id(1) == 0)
    def _():
        st_ref[...] = jnp.zeros_like(st_ref)

    def gate(s, carry):
        rows = pl.ds(pl.multiple_of(s * c, c), c)
        x = jnp.dot(sm_ref[rows, :], w2_ref[...], preferred_element_type=F32,
                    precision=lax.Precision.HIGHEST) + b_ref[...]
        la_ref[rows, :] = -_softplus(-x) * (1.0 / GLA_TAU)
        return carry

    lax.fori_loop(0, rblk // c, gate, 0)

    tri_incl = _iota((c, c), 0) >= _iota((c, c), 1)
    tri2 = jnp.where(_iota((c, 2 * c), 0) >= (_iota((c, 2 * c), 1) % c), 1.0, 0.0).astype(BF16)

    def chunk(n, carry):
        rows = pl.ds(pl.multiple_of(n * c, c), c)
        for h in range(N_HEADS):
            hc = slice(h * hd, (h + 1) * hd)
            vc = slice(h * vd, (h + 1) * vd)
            lh, ll = _split(la_ref[rows, hc])
            gc = _dot(tri2, jnp.concatenate([lh, ll], axis=0))
            gl = gc[c - 1:c, :]
            q = q_ref[rows, hc].astype(F32) * (hd ** -0.5)
            k = k_ref[rows, hc].astype(F32)
            v = v_ref[rows, vc]
            s_old = st_ref[h]
            o = _dot_nt((q * jnp.exp(gc)).astype(BF16), s_old.astype(BF16))
            atts = []
            for i in range(c // SUB):
                rs = slice(i * SUB, (i + 1) * SUB)
                ref_g = gc[i * SUB:i * SUB + 1, :]
                qi = (q[rs] * jnp.exp(gc[rs] - ref_g)).astype(BF16)
                ki = (k * jnp.exp(jnp.minimum(ref_g - gc, 60.0))).astype(BF16)
                atts.append(_dot_nt(qi, ki))
            att = jnp.where(tri_incl, jnp.concatenate(atts, axis=0), 0.0)
            o = o + _dot(att.astype(BF16), v)
            kh = (k * jnp.exp(gl - gc)).astype(BF16)
            st_ref[h] = s_old * jnp.exp(gl) + _dot_tn(v, kh)
            on = o * lax.rsqrt(jnp.mean(o * o, axis=-1, keepdims=True) + RMS_EPS) * on_ref[...]
            zg = z_ref[rows, vc].astype(F32)
            o_ref[rows, vc] = (on * (zg * _sigmoid(zg))).astype(BF16)
        return carry

    lax.fori_loop(0, rblk // c, chunk, 0)


def _gla_call(proj3, small3, w2p, gla_b, onorm, layer, *, rblk=512):
    b, seq, _ = proj3.shape
    rblk = min(rblk, seq)
    wk = N_HEADS * HEAD_DIM
    wv = N_HEADS * GLA_VD
    return pl.pallas_call(
        _gla_kernel,
        out_shape=jax.ShapeDtypeStruct((b, seq, wv), BF16),
        grid=(b, seq // rblk),
        in_specs=[
            pl.BlockSpec((None, rblk, wk), lambda bi, i: (bi, i, OFF_LQ // wk)),
            pl.BlockSpec((None, rblk, wk), lambda bi, i: (bi, i, OFF_LK // wk)),
            pl.BlockSpec((None, rblk, wv), lambda bi, i: (bi, i, OFF_LV // wv)),
            pl.BlockSpec((None, rblk, wv), lambda bi, i: (bi, i, OFF_LZ // wv)),
            pl.BlockSpec((None, rblk, LANES), lambda bi, i: (bi, i, 0)),
            pl.BlockSpec((None, LANES, wk), lambda bi, i: (layer, 0, 0)),
            pl.BlockSpec((None, 1, wk), lambda bi, i: (layer, 0, 0)),
            pl.BlockSpec((None, 1, GLA_VD), lambda bi, i: (layer, 0, 0)),
        ],
        out_specs=pl.BlockSpec((None, rblk, wv), lambda bi, i: (bi, i, 0)),
        scratch_shapes=[pltpu.VMEM((rblk, wk), F32),
                        pltpu.VMEM((N_HEADS, GLA_VD, HEAD_DIM), F32)],
        compiler_params=pltpu.CompilerParams(
            dimension_semantics=("arbitrary", "arbitrary"), vmem_limit_bytes=VMEM_LIMIT),
        name="gla",
    )(proj3, proj3, proj3, proj3, small3, w2p, gla_b, onorm)


def _out_kernel(ysb_ref, ygdn_ref, ygla_ref, mg_ref, mb_ref, psb_ref, pgdn_ref, pgla_ref,
                wout_ref, x_ref, mod_ref, o_ref):
    d = x_ref.shape[1]
    gates = _sigmoid(mg_ref[...].astype(F32) + mb_ref[...])
    merged = gates[:, 0:d] * _dot(ysb_ref[...], psb_ref[...])
    merged = merged + gates[:, d:2 * d] * _dot(ygdn_ref[...], pgdn_ref[...])
    merged = merged + gates[:, 2 * d:3 * d] * _dot(ygla_ref[...], pgla_ref[...])
    o_ref[...] = x_ref[...] + mod_ref[:, 2 * d:3 * d] * _dot(merged.astype(BF16), wout_ref[...])


def _out_call(ysb, ygdn, ygla, proj, merge_b, psb, pgdn, pgla, wout, x2, mod, layer, seq, *, tm):
    t, d = x2.shape
    tm = min(tm, seq)
    rows_per_seq = seq // tm
    wsb, wgdn, wgla = ysb.shape[1], ygdn.shape[1], ygla.shape[1]

    def wspec(k):
        return pl.BlockSpec((None, k, d), lambda i: (layer, 0, 0))

    return pl.pallas_call(
        _out_kernel,
        out_shape=jax.ShapeDtypeStruct((t, d), F32),
        grid=(t // tm,),
        in_specs=[
            pl.BlockSpec((tm, wsb), lambda i: (i, 0)),
            pl.BlockSpec((tm, wgdn), lambda i: (i, 0)),
            pl.BlockSpec((tm, wgla), lambda i: (i, 0)),
            pl.BlockSpec((tm, N_BRANCH * d), lambda i: (i, OFF_MG // (N_BRANCH * d))),
            pl.BlockSpec((None, 1, N_BRANCH * d), lambda i: (layer, 0, 0)),
            wspec(wsb), wspec(wgdn), wspec(wgla), wspec(d),
            pl.BlockSpec((tm, d), lambda i: (i, 0)),
            pl.BlockSpec((None, None, 1, 3 * d), lambda i: (layer, i // rows_per_seq, 0, 0)),
        ],
        out_specs=pl.BlockSpec((tm, d), lambda i: (i, 0)),
        compiler_params=pltpu.CompilerParams(
            dimension_semantics=("arbitrary",), vmem_limit_bytes=VMEM_LIMIT),
        name="merge_out",
    )(ysb, ygdn, ygla, proj, merge_b, psb, pgdn, pgla, wout, x2, mod)


def _reorder_w_in(w_in):
    hw = N_HEADS * HEAD_DIM
    sb = w_in[..., 0:4 * hw]
    dqkv = w_in[..., 4 * hw:7 * hw]
    dz = w_in[..., 7 * hw:8 * hw]
    o = 8 * hw
    db = w_in[..., o:o + N_HEADS]
    da = w_in[..., o + N_HEADS:o + 2 * N_HEADS]
    o += 2 * N_HEADS
    lqk = w_in[..., o:o + 2 * hw]
    lvz = w_in[..., o + 2 * hw:o + 2 * hw + 2 * N_HEADS * GLA_VD]
    o += 2 * hw + 2 * N_HEADS * GLA_VD
    lr = w_in[..., o:o + GLA_RANK]
    mg = w_in[..., o + GLA_RANK:]
    main = jnp.concatenate([mg, dqkv, dz, sb, lqk, lvz], axis=-1).astype(BF16)
    pad = jnp.zeros(w_in.shape[:-1] + (LANES - GLA_RANK - 2 * N_HEADS,), w_in.dtype)
    small = jnp.concatenate([lr, db, da, pad], axis=-1).astype(BF16)
    return main, small


def kernel(x, c, ada_w, ada_b, norm_g, w_in, sb_qnorm, sb_knorm, gdn_conv, gdn_a_log, gdn_dt_bias,
           gdn_onorm, gla_w2, gla_b, gla_onorm, merge_b, proj_sb, proj_gdn, proj_gla, w_out):
    b, seq, d = x.shape
    n_l = w_in.shape[0]
    t = b * seq

    w_main, w_small = _reorder_w_in(w_in)
    assert w_main.shape[-1] == N_MAIN
    mod = _ada_call(c, ada_w, ada_b).reshape(n_l, b, 1, 3 * d)
    hp = jnp.stack([jnp.repeat(gdn_a_log, HEAD_DIM, axis=-1),
                    jnp.repeat(gdn_dt_bias, HEAD_DIM, axis=-1)], axis=1)
    w2p = jnp.pad(gla_w2, ((0, 0), (SM_LR, LANES - SM_LR - GLA_RANK), (0, 0)))
    r3 = lambda a: a.reshape(n_l, 1, a.shape[-1])
    psb, pgdn, pgla, wout = (a.astype(BF16) for a in (proj_sb, proj_gdn, proj_gla, w_out))

    x2 = x.reshape(t, d)
    for layer in range(n_l):
        proj, small = _in_call(x2, mod, r3(norm_g), w_main, w_small, layer, seq, tm=1024, tn=2048)
        proj3 = proj.reshape(b, seq, N_MAIN)
        small3 = small.reshape(b, seq, LANES)
        ysb = _sb_call(proj3, r3(sb_qnorm), r3(sb_knorm), layer)
        ygdn = _gdn_call(proj3, small3, gdn_conv, hp, r3(gdn_onorm), layer)
        ygla = _gla_call(proj3, small3, w2p, r3(gla_b), r3(gla_onorm), layer)
        x2 = _out_call(ysb.reshape(t, -1), ygdn.reshape(t, -1), ygla.reshape(t, -1), proj,
                       r3(merge_b), psb, pgdn, pgla, wout, x2, mod, layer, seq, tm=512)
    return x2.reshape(b, seq, d)
```

```python
import functools

import jax
import jax.numpy as jnp
from jax import lax
from jax.experimental import pallas as pl
from jax.experimental.pallas import tpu as pltpu

F32 = jnp.float32
BF16 = jnp.bfloat16

D_MODEL = 1024
N_HEADS = 4
HEAD_DIM = 128
GLA_VD = 256
GDN_CONV = 4
GLA_RANK = 16
GLA_TAU = 16.0
N_BRANCH = 3
RMS_EPS = 1e-6
L2_EPS = 1e-6
CHUNK = 64
SUB = 16
LANES = 128
VMEM_LIMIT = 56 * 1024 * 1024

OFF_MG = 0
OFF_DQKV = 3072
OFF_DZ = 4608
OFF_SQ, OFF_SK, OFF_SV, OFF_SZ = 5120, 5632, 6144, 6656
OFF_LQ, OFF_LK, OFF_LV, OFF_LZ = 7168, 7680, 8192, 9216
N_MAIN = 10240
SM_LR, SM_DB, SM_DA = 0, 16, 20


def _dot(a, b):
    return jnp.dot(a, b, preferred_element_type=F32)


def _dot_nt(a, b):
    return lax.dot_general(a, b, (((1,), (1,)), ((), ())), preferred_element_type=F32)


def _dot_tn(a, b):
    return lax.dot_general(a, b, (((0,), (0,)), ((), ())), preferred_element_type=F32)


def _split(x):
    hi = x.astype(BF16)
    lo = (x - hi.astype(F32)).astype(BF16)
    return hi, lo


def _sigmoid(x):
    return 1.0 / (1.0 + jnp.exp(-x))


def _softplus(x):
    return jnp.maximum(x, 0.0) + jnp.log(1.0 + jnp.exp(-jnp.abs(x)))


def _iota(shape, axis):
    return lax.broadcasted_iota(jnp.int32, shape, axis)


def _ada_kernel(c_ref, w_ref, b_ref, o_ref):
    c = c_ref[...]
    ca = c * _sigmoid(c)
    o_ref[...] = jnp.dot(ca, w_ref[...], preferred_element_type=F32,
                         precision=lax.Precision.HIGHEST) + b_ref[...]


def _ada_call(c, ada_w, ada_b):
    n_l, d, d3 = ada_w.shape
    b = c.shape[0]
    tn = 768
    return pl.pallas_call(
        _ada_kernel,
        out_shape=jax.ShapeDtypeStruct((n_l, b, d3), F32),
        grid=(n_l, d3 // tn),
        in_specs=[
            pl.BlockSpec((b, d), lambda l, j: (0, 0)),
            pl.BlockSpec((None, d, tn), lambda l, j: (l, 0, j)),
            pl.BlockSpec((None, 1, tn), lambda l, j: (l, 0, j)),
        ],
        out_specs=pl.BlockSpec((None, b, tn), lambda l, j: (l, 0, j)),
        compiler_params=pltpu.CompilerParams(
            dimension_semantics=("arbitrary", "arbitrary"), vmem_limit_bytes=VMEM_LIMIT),
        name="adaln_mod",
    )(c, ada_w, ada_b.reshape(n_l, 1, d3))


def _in_kernel(x_ref, mod_ref, g_ref, w_ref, ws_ref, out_ref, small_ref, h_ref, *, slab):
    tm, d = x_ref.shape

    @pl.when(pl.program_id(1) == 0)
    def _():
        shift = mod_ref[:, 0:d]
        scale1 = 1.0 + mod_ref[:, d:2 * d]
        gain = g_ref[...]

        def body(s, carry):
            rows = pl.ds(pl.multiple_of(s * slab, slab), slab)
            x = x_ref[rows, :]
            y = x * lax.rsqrt(jnp.mean(x * x, axis=-1, keepdims=True) + RMS_EPS) * gain
            h_ref[rows, :] = (y * scale1 + shift).astype(BF16)
            return carry

        lax.fori_loop(0, tm // slab, body, 0)
        small_ref[...] = _dot(h_ref[...], ws_ref[...])

    out_ref[...] = _dot(h_ref[...], w_ref[...]).astype(BF16)


def _in_call(x2, mod, norm_g, w_main, w_small, layer, seq, *, tm, tn):
    t, d = x2.shape
    n_main = w_main.shape[-1]
    tm = min(tm, seq)
    rows_per_seq = seq // tm
    kern = functools.partial(_in_kernel, slab=min(128, tm))
    return pl.pallas_call(
        kern,
        out_shape=(jax.ShapeDtypeStruct((t, n_main), BF16),
                   jax.ShapeDtypeStruct((t, LANES), F32)),
        grid=(t // tm, n_main // tn),
        in_specs=[
            pl.BlockSpec((tm, d), lambda i, j: (i, 0)),
            pl.BlockSpec((None, None, 1, 3 * d), lambda i, j: (layer, i // rows_per_seq, 0, 0)),
            pl.BlockSpec((None, 1, d), lambda i, j: (layer, 0, 0)),
            pl.BlockSpec((None, d, tn), lambda i, j: (layer, 0, j)),
            pl.BlockSpec((None, d, LANES), lambda i, j: (layer, 0, 0)),
        ],
        out_specs=(pl.BlockSpec((tm, tn), lambda i, j: (i, j)),
                   pl.BlockSpec((tm, LANES), lambda i, j: (i, 0))),
        scratch_shapes=[pltpu.VMEM((tm, d), BF16)],
        compiler_params=pltpu.CompilerParams(
            dimension_semantics=("arbitrary", "arbitrary"), vmem_limit_bytes=VMEM_LIMIT),
        name="in_proj",
    )(x2, mod, norm_g, w_main, w_small)


def _sb_kernel(q_ref, k_ref, v_ref, z_ref, qw_ref, kw_ref, o_ref, kn_ref, qn_ref, acc_ref, run_ref,
               *, tq):
    seq = q_ref.shape[0]
    hd = HEAD_DIM
    tk = LANES
    assert tq == 2 * tk
    scale = hd ** -0.5
    heads = range(N_HEADS)
    hcs = [slice(h * hd, (h + 1) * hd) for h in heads]

    def rms(x, w):
        return x * lax.rsqrt(jnp.mean(x * x, axis=-1, keepdims=True) + RMS_EPS) * w

    def knorm(s, carry):
        rows = pl.ds(pl.multiple_of(s * tq, tq), tq)
        for h in heads:
            kn_ref[rows, hcs[h]] = rms(k_ref[rows, hcs[h]].astype(F32), kw_ref[...]).astype(BF16)
        return carry

    lax.fori_loop(0, seq // tq, knorm, 0)

    jj = _iota((2 * tk, 2 * tk), 0) % tk
    ss = _iota((2 * tk, 2 * tk), 1)
    tail_w = jnp.where((ss >= tk) | (jj > ss), 1.0, 0.0).astype(BF16)

    def step(k0, qpos):
        keys = pl.ds(k0, 2 * tk)
        z = [_dot_nt(qn_ref[h], kn_ref[keys, hcs[h]]) for h in heads]
        lg = [-_softplus(x) for x in z]
        if qpos is not None:
            mask = (k0 + _iota((tq, 2 * tk), 1)) < qpos
            lg = [jnp.where(mask, x, 0.0) for x in lg]
        parts = [_split(x) for x in lg]
        s_hi = [_dot(jnp.concatenate([hi[:, tk:], lo[:, tk:]], axis=1), tail_w) for hi, lo in parts]
        s_lo = [_dot(jnp.concatenate([hi[:, :tk], lo[:, :tk]], axis=1), tail_w) for hi, lo in parts]
        ws = []
        for h in heads:
            run = run_ref[h]
            run_mid = run + s_hi[h][:, tk:]
            off = jnp.concatenate([s_lo[h][:, :tk] + run_mid, s_hi[h][:, :tk] + run], axis=1)
            w = jnp.exp(lg[h] + z[h] + off)
            if qpos is not None:
                w = jnp.where(mask, w, 0.0)
            ws.append(w.astype(BF16))
            run_ref[h] = run_mid + s_lo[h][:, tk:]
        for h in heads:
            acc_ref[h] += _dot(ws[h], v_ref[keys, hcs[h]])

    def q_block(i, carry):
        r0 = pl.multiple_of(i * tq, tq)
        rows = pl.ds(r0, tq)
        for h in heads:
            qn_ref[h] = (rms(q_ref[rows, hcs[h]].astype(F32), qw_ref[...]) * scale).astype(BF16)
            acc_ref[h] = jnp.zeros((tq, hd), F32)
            run_ref[h] = jnp.zeros((tq, tk), F32)
        step(r0, r0 + _iota((tq, 2 * tk), 0))

        def below(n, c):
            step(pl.multiple_of((i - 1 - n) * tq, tq), None)
            return c

        lax.fori_loop(0, i, below, 0)
        for h in heads:
            zg = z_ref[rows, hcs[h]].astype(F32)
            o_ref[rows, hcs[h]] = (acc_ref[h] * (zg * _sigmoid(zg))).astype(BF16)
        return carry

    lax.fori_loop(0, seq // tq, q_block, 0)


def _sb_call(proj3, qw, kw, layer, *, tq=256):
    b, seq, _ = proj3.shape
    hd = HEAD_DIM
    w = N_HEADS * hd

    def col(off):
        return pl.BlockSpec((None, seq, w), lambda bi: (bi, 0, off // w))

    wspec = pl.BlockSpec((None, 1, hd), lambda bi: (layer, 0, 0))
    return pl.pallas_call(
        functools.partial(_sb_kernel, tq=tq),
        out_shape=jax.ShapeDtypeStruct((b, seq, w), BF16),
        grid=(b,),
        in_specs=[col(OFF_SQ), col(OFF_SK), col(OFF_SV), col(OFF_SZ), wspec, wspec],
        out_specs=pl.BlockSpec((None, seq, w), lambda bi: (bi, 0, 0)),
        scratch_shapes=[pltpu.VMEM((seq, w), BF16),
                        pltpu.VMEM((N_HEADS, tq, hd), BF16),
                        pltpu.VMEM((N_HEADS, tq, hd), F32),
                        pltpu.VMEM((N_HEADS, tq, LANES), F32)],
        compiler_params=pltpu.CompilerParams(
            dimension_semantics=("arbitrary",), vmem_limit_bytes=VMEM_LIMIT),
        name="sb_attn",
    )(proj3, proj3, proj3, proj3, qw, kw)


def _unit_lower_inverse(a_list, eye):
    t = [eye - a for a in a_list]
    p = list(a_list)
    for _ in range(5):
        pb = [x.astype(BF16) for x in p]
        p = [_dot(x, x) for x in pb]
        t = [x + _dot(x.astype(BF16), y.astype(BF16)) for x, y in zip(t, p)]
    return t


def _gdn_kernel(qkv_ref, prev_ref, dz_ref, sm_ref, conv_ref, hp_ref, on_ref, o_ref,
                xs_ref, st_ref):
    rblk = qkv_ref.shape[0]
    hd = HEAD_DIM
    c = CHUNK
    halo = prev_ref.shape[0]
    blk = pl.program_id(1)

    @pl.when(blk == 0)
    def _():
        st_ref[...] = jnp.zeros_like(st_ref)
        xs_ref[0:halo, :] = jnp.zeros((halo, xs_ref.shape[1]), F32)

    @pl.when(blk > 0)
    def _():
        xs_ref[0:halo, :] = prev_ref[...].astype(F32)

    def stage(s, carry):
        rows = pl.multiple_of(s * c, c)
        xs_ref[pl.ds(halo + rows, c), :] = qkv_ref[pl.ds(rows, c), :].astype(F32)
        return carry

    lax.fori_loop(0, rblk // c, stage, 0)

    ri = _iota((c, c), 0)
    ci = _iota((c, c), 1)
    tri_incl = ri >= ci
    tri_strict = ri > ci
    eye = jnp.where(ri == ci, 1.0, 0.0).astype(F32)
    tri2 = jnp.where(_iota((c, 2 * c), 0) >= (_iota((c, 2 * c), 1) % c), 1.0, 0.0).astype(BF16)
    later = jnp.where(_iota((c, LANES), 0) > _iota((c, LANES), 1), 1.0, 0.0).astype(F32)

    def chunk(n, carry):
        r0 = pl.multiple_of(n * c, c)
        rows = pl.ds(r0, c)
        win = pl.ds(r0, c + halo)
        sm = sm_ref[rows, :]
        heads = range(N_HEADS)
        hcs = [slice(h * hd, (h + 1) * hd) for h in heads]

        def conv(col0):
            w = xs_ref[win, col0:col0 + hd]
            cw = conv_ref[:, col0:col0 + hd]
            y = w[halo:] * cw[GDN_CONV - 1:GDN_CONV]
            for j in range(1, GDN_CONV):
                y = y + pltpu.roll(w, j, axis=0)[halo:] * cw[GDN_CONV - 1 - j:GDN_CONV - j]
            return y * _sigmoid(y)

        q = [conv(h * hd) for h in heads]
        k = [conv(N_HEADS * hd + h * hd) for h in heads]
        v = [conv(2 * N_HEADS * hd + h * hd) for h in heads]
        q = [x * (lax.rsqrt(jnp.sum(x * x, axis=-1, keepdims=True) + L2_EPS) * (hd ** -0.5))
             for x in q]
        k = [x * lax.rsqrt(jnp.sum(x * x, axis=-1, keepdims=True) + L2_EPS) for x in k]
        beta = [_sigmoid(jnp.broadcast_to(sm[:, SM_DB + h:SM_DB + h + 1], (c, hd))) for h in heads]
        g = [-jnp.exp(hp_ref[0:1, hcs[h]])
             * _softplus(jnp.broadcast_to(sm[:, SM_DA + h:SM_DA + h + 1], (c, hd)) + hp_ref[1:2, hcs[h]])
             for h in heads]

        cum = []
        for h in heads:
            xh, xl = _split(jnp.concatenate([g[h], g[h] * later], axis=1))
            cum.append(_dot(tri2, jnp.concatenate([xh, xl], axis=0)))
        gam = [x[:, :hd] for x in cum]
        decay = [jnp.where(tri_incl, jnp.exp(x[:, hd:hd + c]), 0.0) for x in cum]
        eg = [jnp.exp(x) for x in gam]
        gl = [x[c - 1:c, :] for x in gam]

        kb = [x.astype(BF16) for x in k]
        kk = [_dot_nt(x, x) for x in kb]
        qk = [_dot_nt(q[h].astype(BF16), kb[h]) * decay[h] for h in heads]
        a = [jnp.where(tri_strict, beta[h][:, :c] * kk[h] * decay[h], 0.0) for h in heads]
        tinv = _unit_lower_inverse(a, eye)
        sol = [_dot(tinv[h].astype(BF16),
                    jnp.concatenate([v[h] * beta[h], k[h] * (beta[h] * eg[h])], axis=1).astype(BF16))
               for h in heads]

        s_old = [st_ref[h] for h in heads]
        r = [_dot(jnp.concatenate([sol[h][:, hd:], q[h] * eg[h]], axis=0).astype(BF16),
                  s_old[h].astype(BF16)) for h in heads]
        ub = [(sol[h][:, :hd] - r[h][:c]).astype(BF16) for h in heads]
        o = [r[h][c:] + _dot(qk[h].astype(BF16), ub[h]) for h in heads]
        for h in heads:
            kd = (k[h] * jnp.exp(gl[h] - gam[h])).astype(BF16)
            st_ref[h] = s_old[h] * jnp.exp(gl[h]) + _dot_tn(kd, ub[h])
        for h in heads:
            on = (o[h] * lax.rsqrt(jnp.mean(o[h] * o[h], axis=-1, keepdims=True) + RMS_EPS)
                  * on_ref[...])
            zg = dz_ref[rows, hcs[h]].astype(F32)
            o_ref[rows, hcs[h]] = (on * (zg * _sigmoid(zg))).astype(BF16)
        return carry

    lax.fori_loop(0, rblk // c, chunk, 0)


def _gdn_call(proj3, small3, conv_w, hp, onorm, layer, *, rblk=512, halo=16):
    b, seq, _ = proj3.shape
    rblk = min(rblk, seq)
    w3 = 3 * N_HEADS * HEAD_DIM
    wz = N_HEADS * HEAD_DIM
    per = rblk // halo
    return pl.pallas_call(
        _gdn_kernel,
        out_shape=jax.ShapeDtypeStruct((b, seq, wz), BF16),
        grid=(b, seq // rblk),
        in_specs=[
            pl.BlockSpec((None, rblk, w3), lambda bi, i: (bi, i, OFF_DQKV // w3)),
            pl.BlockSpec((None, halo, w3),
                         lambda bi, i: (bi, jnp.maximum(i * per - 1, 0), OFF_DQKV // w3)),
            pl.BlockSpec((None, rblk, wz), lambda bi, i: (bi, i, OFF_DZ // wz)),
            pl.BlockSpec((None, rblk, LANES), lambda bi, i: (bi, i, 0)),
            pl.BlockSpec((None, GDN_CONV, w3), lambda bi, i: (layer, 0, 0)),
            pl.BlockSpec((None, 2, wz), lambda bi, i: (layer, 0, 0)),
            pl.BlockSpec((None, 1, HEAD_DIM), lambda bi, i: (layer, 0, 0)),
        ],
        out_specs=pl.BlockSpec((None, rblk, wz), lambda bi, i: (bi, i, 0)),
        scratch_shapes=[pltpu.VMEM((rblk + halo, w3), F32),
                        pltpu.VMEM((N_HEADS, HEAD_DIM, HEAD_DIM), F32)],
        compiler_params=pltpu.CompilerParams(
            dimension_semantics=("arbitrary", "arbitrary"), vmem_limit_bytes=VMEM_LIMIT),
        name="gdn",
    )(proj3, proj3, proj3, small3, conv_w, hp, onorm)


def _gla_kernel(q_ref, k_ref, v_ref, z_ref, sm_ref, w2_ref, b_ref, on_ref, o_ref, la_ref, st_ref):
    rblk = q_ref.shape[0]
    hd = HEAD_DIM
    vd = GLA_VD
    c = CHUNK

    @pl.when(pl.program_id(1) == 0)
    def _():
        st_ref[...] = jnp.zeros_like(st_ref)

    def gate(s, carry):
        rows = pl.ds(pl.multiple_of(s * c, c), c)
        x = jnp.dot(sm_ref[rows, :], w2_ref[...], preferred_element_type=F32,
                    precision=lax.Precision.HIGHEST) + b_ref[...]
        la_ref[rows, :] = -_softplus(-x) * (1.0 / GLA_TAU)
        return carry

    lax.fori_loop(0, rblk // c, gate, 0)

    tri_incl = _iota((c, c), 0) >= _iota((c, c), 1)
    tri2 = jnp.where(_iota((c, 2 * c), 0) >= (_iota((c, 2 * c), 1) % c), 1.0, 0.0).astype(BF16)

    def chunk(n, carry):
        rows = pl.ds(pl.multiple_of(n * c, c), c)
        heads = range(N_HEADS)
        hcs = [slice(h * hd, (h + 1) * hd) for h in heads]
        vcs = [slice(h * vd, (h + 1) * vd) for h in heads]
        gc = []
        for h in heads:
            lh, ll = _split(la_ref[rows, hcs[h]])
            gc.append(_dot(tri2, jnp.concatenate([lh, ll], axis=0)))
        gl = [x[c - 1:c, :] for x in gc]
        q = [q_ref[rows, hcs[h]].astype(F32) * (hd ** -0.5) for h in heads]
        k = [k_ref[rows, hcs[h]].astype(F32) for h in heads]
        v = [v_ref[rows, vcs[h]] for h in heads]
        s_old = [st_ref[h] for h in heads]
        o = [_dot_nt((q[h] * jnp.exp(gc[h])).astype(BF16), s_old[h].astype(BF16)) for h in heads]
        att = []
        for h in heads:
            parts = []
            for i in range(c // SUB):
                rs = slice(i * SUB, (i + 1) * SUB)
                ref_g = gc[h][i * SUB:i * SUB + 1, :]
                qi = (q[h][rs] * jnp.exp(gc[h][rs] - ref_g)).astype(BF16)
                ki = (k[h] * jnp.exp(jnp.minimum(ref_g - gc[h], 60.0))).astype(BF16)
                parts.append(_dot_nt(qi, ki))
            att.append(jnp.where(tri_incl, jnp.concatenate(parts, axis=0), 0.0))
        o = [o[h] + _dot(att[h].astype(BF16), v[h]) for h in heads]
        for h in heads:
            kh = (k[h] * jnp.exp(gl[h] - gc[h])).astype(BF16)
            st_ref[h] = s_old[h] * jnp.exp(gl[h]) + _dot_tn(v[h], kh)
        for h in heads:
            on = (o[h] * lax.rsqrt(jnp.mean(o[h] * o[h], axis=-1, keepdims=True) + RMS_EPS)
                  * on_ref[...])
            zg = z_ref[rows, vcs[h]].astype(F32)
            o_ref[rows, vcs[h]] = (on * (zg * _sigmoid(zg))).astype(BF16)
        return carry

    lax.fori_loop(0, rblk // c, chunk, 0)


def _gla_call(proj3, small3, w2p, gla_b, onorm, layer, *, rblk=512):
    b, seq, _ = proj3.shape
    rblk = min(rblk, seq)
    wk = N_HEADS * HEAD_DIM
    wv = N_HEADS * GLA_VD
    return pl.pallas_call(
        _gla_kernel,
        out_shape=jax.ShapeDtypeStruct((b, seq, wv), BF16),
        grid=(b, seq // rblk),
        in_specs=[
            pl.BlockSpec((None, rblk, wk), lambda bi, i: (bi, i, OFF_LQ // wk)),
            pl.BlockSpec((None, rblk, wk), lambda bi, i: (bi, i, OFF_LK // wk)),
            pl.BlockSpec((None, rblk, wv), lambda bi, i: (bi, i, OFF_LV // wv)),
            pl.BlockSpec((None, rblk, wv), lambda bi, i: (bi, i, OFF_LZ // wv)),
            pl.BlockSpec((None, rblk, LANES), lambda bi, i: (bi, i, 0)),
            pl.BlockSpec((None, LANES, wk), lambda bi, i: (layer, 0, 0)),
            pl.BlockSpec((None, 1, wk), lambda bi, i: (layer, 0, 0)),
            pl.BlockSpec((None, 1, GLA_VD), lambda bi, i: (layer, 0, 0)),
        ],
        out_specs=pl.BlockSpec((None, rblk, wv), lambda bi, i: (bi, i, 0)),
        scratch_shapes=[pltpu.VMEM((rblk, wk), F32),
                        pltpu.VMEM((N_HEADS, GLA_VD, HEAD_DIM), F32)],
        compiler_params=pltpu.CompilerParams(
            dimension_semantics=("arbitrary", "arbitrary"), vmem_limit_bytes=VMEM_LIMIT),
        name="gla",
    )(proj3, proj3, proj3, proj3, small3, w2p, gla_b, onorm)


def _out_kernel(ysb_ref, ygdn_ref, ygla_ref, mg_ref, mb_ref, psb_ref, pgdn_ref, pgla_ref,
                wout_ref, x_ref, mod_ref, o_ref):
    d = x_ref.shape[1]
    gates = _sigmoid(mg_ref[...].astype(F32) + mb_ref[...])
    merged = gates[:, 0:d] * _dot(ysb_ref[...], psb_ref[...])
    merged = merged + gates[:, d:2 * d] * _dot(ygdn_ref[...], pgdn_ref[...])
    merged = merged + gates[:, 2 * d:3 * d] * _dot(ygla_ref[...], pgla_ref[...])
    o_ref[...] = x_ref[...] + mod_ref[:, 2 * d:3 * d] * _dot(merged.astype(BF16), wout_ref[...])


def _out_call(ysb, ygdn, ygla, proj, merge_b, psb, pgdn, pgla, wout, x2, mod, layer, seq, *, tm):
    t, d = x2.shape
    tm = min(tm, seq)
    rows_per_seq = seq // tm
    wsb, wgdn, wgla = ysb.shape[1], ygdn.shape[1], ygla.shape[1]

    def wspec(k):
        return pl.BlockSpec((None, k, d), lambda i: (layer, 0, 0))

    return pl.pallas_call(
        _out_kernel,
        out_shape=jax.ShapeDtypeStruct((t, d), F32),
        grid=(t // tm,),
        in_specs=[
            pl.BlockSpec((tm, wsb), lambda i: (i, 0)),
            pl.BlockSpec((tm, wgdn), lambda i: (i, 0)),
            pl.BlockSpec((tm, wgla), lambda i: (i, 0)),
            pl.BlockSpec((tm, N_BRANCH * d), lambda i: (i, OFF_MG // (N_BRANCH * d))),
            pl.BlockSpec((None, 1, N_BRANCH * d), lambda i: (layer, 0, 0)),
            wspec(wsb), wspec(wgdn), wspec(wgla), wspec(d),
            pl.BlockSpec((tm, d), lambda i: (i, 0)),
            pl.BlockSpec((None, None, 1, 3 * d), lambda i: (layer, i // rows_per_seq, 0, 0)),
        ],
        out_specs=pl.BlockSpec((tm, d), lambda i: (i, 0)),
        compiler_params=pltpu.CompilerParams(
            dimension_semantics=("arbitrary",), vmem_limit_bytes=VMEM_LIMIT),
        name="merge_out",
    )(ysb, ygdn, ygla, proj, merge_b, psb, pgdn, pgla, wout, x2, mod)


def _reorder_w_in(w_in):
    hw = N_HEADS * HEAD_DIM
    sb = w_in[..., 0:4 * hw]
    dqkv = w_in[..., 4 * hw:7 * hw]
    dz = w_in[..., 7 * hw:8 * hw]
    o = 8 * hw
    db = w_in[..., o:o + N_HEADS]
    da = w_in[..., o + N_HEADS:o + 2 * N_HEADS]
    o += 2 * N_HEADS
    lqk = w_in[..., o:o + 2 * hw]
    lvz = w_in[..., o + 2 * hw:o + 2 * hw + 2 * N_HEADS * GLA_VD]
    o += 2 * hw + 2 * N_HEADS * GLA_VD
    lr = w_in[..., o:o + GLA_RANK]
    mg = w_in[..., o + GLA_RANK:]
    main = jnp.concatenate([mg, dqkv, dz, sb, lqk, lvz], axis=-1).astype(BF16)
    pad = jnp.zeros(w_in.shape[:-1] + (LANES - GLA_RANK - 2 * N_HEADS,), w_in.dtype)
    small = jnp.concatenate([lr, db, da, pad], axis=-1).astype(BF16)
    return main, small


def kernel(x, c, ada_w, ada_b, norm_g, w_in, sb_qnorm, sb_knorm, gdn_conv, gdn_a_log, gdn_dt_bias,
           gdn_onorm, gla_w2, gla_b, gla_onorm, merge_b, proj_sb, proj_gdn, proj_gla, w_out):
    b, seq, d = x.shape
    n_l = w_in.shape[0]
    t = b * seq

    w_main, w_small = _reorder_w_in(w_in)
    assert w_main.shape[-1] == N_MAIN
    mod = _ada_call(c, ada_w, ada_b).reshape(n_l, b, 1, 3 * d)
    hp = jnp.stack([jnp.repeat(gdn_a_log, HEAD_DIM, axis=-1),
                    jnp.repeat(gdn_dt_bias, HEAD_DIM, axis=-1)], axis=1)
    w2p = jnp.pad(gla_w2, ((0, 0), (SM_LR, LANES - SM_LR - GLA_RANK), (0, 0)))
    r3 = lambda a: a.reshape(n_l, 1, a.shape[-1])
    psb, pgdn, pgla, wout = (a.astype(BF16) for a in (proj_sb, proj_gdn, proj_gla, w_out))

    x2 = x.reshape(t, d)
    for layer in range(n_l):
        proj, small = _in_call(x2, mod, r3(norm_g), w_main, w_small, layer, seq, tm=1024, tn=2048)
        proj3 = proj.reshape(b, seq, N_MAIN)
        small3 = small.reshape(b, seq, LANES)
        ysb = _sb_call(proj3, r3(sb_qnorm), r3(sb_knorm), layer)
        ygdn = _gdn_call(proj3, small3, gdn_conv, hp, r3(gdn_onorm), layer)
        ygla = _gla_call(proj3, small3, w2p, r3(gla_b), r3(gla_onorm), layer)
        x2 = _out_call(ysb.reshape(t, -1), ygdn.reshape(t, -1), ygla.reshape(t, -1), proj,
                       r3(merge_b), psb, pgdn, pgla, wout, x2, mod, layer, seq, tm=512)
    return x2.reshape(b, seq, d)
```

```python
import functools

import jax
import jax.numpy as jnp
from jax import lax
from jax.experimental import pallas as pl
from jax.experimental.pallas import tpu as pltpu

F32 = jnp.float32
BF16 = jnp.bfloat16

D_MODEL = 1024
N_HEADS = 4
HEAD_DIM = 128
GLA_VD = 256
GDN_CONV = 4
GLA_RANK = 16
GLA_TAU = 16.0
N_BRANCH = 3
RMS_EPS = 1e-6
L2_EPS = 1e-6
CHUNK = 64
GDN_UNROLL = 4
GLA_UNROLL = 2
SUB = 16
LANES = 128
LOG2E = 1.4426950408889634
VMEM_LIMIT = 56 * 1024 * 1024

OFF_MG = 0
OFF_DQKV = 3072
OFF_DZ = 4608
OFF_SQ, OFF_SK, OFF_SV, OFF_SZ = 5120, 5632, 6144, 6656
OFF_LQ, OFF_LK, OFF_LV, OFF_LZ = 7168, 7680, 8192, 9216
N_MAIN = 10240
SM_LR, SM_DB, SM_DA = 0, 16, 20


def _dot(a, b):
    return jnp.dot(a, b, preferred_element_type=F32)


def _dot_nt(a, b):
    return lax.dot_general(a, b, (((1,), (1,)), ((), ())), preferred_element_type=F32)


def _dot_tn(a, b):
    return lax.dot_general(a, b, (((0,), (0,)), ((), ())), preferred_element_type=F32)


def _split(x):
    hi = x.astype(BF16)
    lo = (x - hi.astype(F32)).astype(BF16)
    return hi, lo


def _sigmoid(x):
    return 1.0 / (1.0 + jnp.exp(-x))


def _softplus(x):
    return jnp.maximum(x, 0.0) + jnp.log(1.0 + jnp.exp(-jnp.abs(x)))


def _iota(shape, axis):
    return lax.broadcasted_iota(jnp.int32, shape, axis)


def _ada_kernel(c_ref, w_ref, b_ref, o_ref):
    c = c_ref[...]
    ca = c * _sigmoid(c)
    o_ref[...] = jnp.dot(ca, w_ref[...], preferred_element_type=F32,
                         precision=lax.Precision.HIGHEST) + b_ref[...]


def _ada_call(c, ada_w, ada_b):
    n_l, d, d3 = ada_w.shape
    b = c.shape[0]
    tn = 768
    return pl.pallas_call(
        _ada_kernel,
        out_shape=jax.ShapeDtypeStruct((n_l, b, d3), F32),
        grid=(n_l, d3 // tn),
        in_specs=[
            pl.BlockSpec((b, d), lambda l, j: (0, 0)),
            pl.BlockSpec((None, d, tn), lambda l, j: (l, 0, j)),
            pl.BlockSpec((None, 1, tn), lambda l, j: (l, 0, j)),
        ],
        out_specs=pl.BlockSpec((None, b, tn), lambda l, j: (l, 0, j)),
        compiler_params=pltpu.CompilerParams(
            dimension_semantics=("arbitrary", "arbitrary"), vmem_limit_bytes=VMEM_LIMIT),
        name="adaln_mod",
    )(c, ada_w, ada_b.reshape(n_l, 1, d3))


def _in_kernel(x_ref, mod_ref, g_ref, w_ref, ws_ref, out_ref, small_ref, h_ref, *, slab):
    tm, d = x_ref.shape

    @pl.when(pl.program_id(1) == 0)
    def _():
        shift = mod_ref[:, 0:d]
        scale1 = 1.0 + mod_ref[:, d:2 * d]
        gain = g_ref[...]

        def body(s, carry):
            rows = pl.ds(pl.multiple_of(s * slab, slab), slab)
            x = x_ref[rows, :]
            y = x * lax.rsqrt(jnp.mean(x * x, axis=-1, keepdims=True) + RMS_EPS) * gain
            h_ref[rows, :] = (y * scale1 + shift).astype(BF16)
            return carry

        lax.fori_loop(0, tm // slab, body, 0)
        small_ref[...] = _dot(h_ref[...], ws_ref[...])

    out_ref[...] = _dot(h_ref[...], w_ref[...]).astype(BF16)


def _in_call(x2, mod, norm_g, w_main, w_small, layer, seq, *, tm, tn):
    t, d = x2.shape
    n_main = w_main.shape[-1]
    tm = min(tm, seq)
    rows_per_seq = seq // tm
    kern = functools.partial(_in_kernel, slab=min(128, tm))
    return pl.pallas_call(
        kern,
        out_shape=(jax.ShapeDtypeStruct((t, n_main), BF16),
                   jax.ShapeDtypeStruct((t, LANES), F32)),
        grid=(t // tm, n_main // tn),
        in_specs=[
            pl.BlockSpec((tm, d), lambda i, j: (i, 0)),
            pl.BlockSpec((None, None, 1, 3 * d), lambda i, j: (layer, i // rows_per_seq, 0, 0)),
            pl.BlockSpec((None, 1, d), lambda i, j: (layer, 0, 0)),
            pl.BlockSpec((None, d, tn), lambda i, j: (layer, 0, j)),
            pl.BlockSpec((None, d, LANES), lambda i, j: (layer, 0, 0)),
        ],
        out_specs=(pl.BlockSpec((tm, tn), lambda i, j: (i, j)),
                   pl.BlockSpec((tm, LANES), lambda i, j: (i, 0))),
        scratch_shapes=[pltpu.VMEM((tm, d), BF16)],
        compiler_params=pltpu.CompilerParams(
            dimension_semantics=("arbitrary", "arbitrary"), vmem_limit_bytes=VMEM_LIMIT),
        name="in_proj",
    )(x2, mod, norm_g, w_main, w_small)


def _sb_kernel(q_ref, k_ref, v_ref, z_ref, qw_ref, kw_ref, o_ref, kn_ref, qn_ref, acc_ref, run_ref,
               *, tq):
    seq = q_ref.shape[0]
    hd = HEAD_DIM
    scale = hd ** -0.5
    heads = range(N_HEADS)
    hcs = [slice(h * hd, (h + 1) * hd) for h in heads]

    def rms(x, w):
        return x * lax.rsqrt(jnp.mean(x * x, axis=-1, keepdims=True) + RMS_EPS) * w

    def knorm(s, carry):
        rows = pl.ds(pl.multiple_of(s * tq, tq), tq)
        for h in heads:
            kn_ref[rows, hcs[h]] = rms(k_ref[rows, hcs[h]].astype(F32), kw_ref[...]).astype(BF16)
        return carry

    lax.fori_loop(0, seq // tq, knorm, 0)

    later = jnp.where(_iota((tq, tq), 0) > _iota((tq, tq), 1), 1.0, 0.0).astype(BF16)
    sign = jnp.int32(-2 ** 31)

    def step(k0, qpos):
        keys = pl.ds(k0, tq)
        z = [_dot_nt(qn_ref[h], kn_ref[keys, hcs[h]]) for h in heads]
        t = [jnp.log2(1.0 + jnp.exp2(pltpu.bitcast(pltpu.bitcast(x, jnp.int32) | sign, F32))) for x in z]
        lb = [jnp.minimum(x, 0.0) - y for x, y in zip(z, t)]
        lg = [x - y for x, y in zip(lb, z)]
        if qpos is not None:
            mask = (k0 + _iota((tq, tq), 1)) < qpos
            lg = [jnp.where(mask, x, 0.0) for x in lg]
        tail = [_dot(x.astype(BF16), later) for x in lg]
        ws = []
        for h in heads:
            run = run_ref[h]
            w = jnp.exp2(lb[h] + tail[h] + run)
            if qpos is not None:
                w = jnp.where(mask, w, 0.0)
            ws.append(w.astype(BF16))
            run_ref[h] = run + jnp.sum(lg[h], axis=1, keepdims=True)
        for h in heads:
            acc_ref[h] += _dot(ws[h], v_ref[keys, hcs[h]])

    def q_block(i, carry):
        r0 = pl.multiple_of(i * tq, tq)
        rows = pl.ds(r0, tq)
        for h in heads:
            qn_ref[h] = (rms(q_ref[rows, hcs[h]].astype(F32), qw_ref[...]) * (scale * LOG2E)).astype(BF16)
            acc_ref[h] = jnp.zeros((tq, hd), F32)
            run_ref[h] = jnp.zeros((tq, 1), F32)
        step(r0, r0 + _iota((tq, tq), 0))

        def below(n, c):
            step(pl.multiple_of((i - 1 - n) * tq, tq), None)
            return c

        lax.fori_loop(0, i, below, 0)
        for h in heads:
            zg = z_ref[rows, hcs[h]].astype(F32)
            o_ref[rows, hcs[h]] = (acc_ref[h] * (zg * _sigmoid(zg))).astype(BF16)
        return carry

    lax.fori_loop(0, seq // tq, q_block, 0)


def _sb_call(proj3, qw, kw, layer, *, tq=256):
    b, seq, _ = proj3.shape
    hd = HEAD_DIM
    w = N_HEADS * hd

    def col(off):
        return pl.BlockSpec((None, seq, w), lambda bi: (bi, 0, off // w))

    wspec = pl.BlockSpec((None, 1, hd), lambda bi: (layer, 0, 0))
    return pl.pallas_call(
        functools.partial(_sb_kernel, tq=tq),
        out_shape=jax.ShapeDtypeStruct((b, seq, w), BF16),
        grid=(b,),
        in_specs=[col(OFF_SQ), col(OFF_SK), col(OFF_SV), col(OFF_SZ), wspec, wspec],
        out_specs=pl.BlockSpec((None, seq, w), lambda bi: (bi, 0, 0)),
        scratch_shapes=[pltpu.VMEM((seq, w), BF16),
                        pltpu.VMEM((N_HEADS, tq, hd), BF16),
                        pltpu.VMEM((N_HEADS, tq, hd), F32),
                        pltpu.VMEM((N_HEADS, tq, 1), F32)],
        compiler_params=pltpu.CompilerParams(
            dimension_semantics=("arbitrary",), vmem_limit_bytes=VMEM_LIMIT),
        name="sb_attn",
    )(proj3, proj3, proj3, proj3, qw, kw)


def _unit_lower_inverse(a_list, eye):
    t = [eye - a for a in a_list]
    p = list(a_list)
    for _ in range(5):
        pb = [x.astype(BF16) for x in p]
        p = [_dot(x, x) for x in pb]
        t = [x + _dot(x.astype(BF16), y.astype(BF16)) for x, y in zip(t, p)]
    return t


def _gdn_kernel(qkv_ref, prev_ref, dz_ref, sm_ref, conv_ref, hp_ref, on_ref, o_ref,
                xs_ref, st_ref):
    rblk = qkv_ref.shape[0]
    hd = HEAD_DIM
    c = CHUNK
    halo = prev_ref.shape[0]
    blk = pl.program_id(1)

    @pl.when(blk == 0)
    def _():
        st_ref[...] = jnp.zeros_like(st_ref)
        xs_ref[0:halo, :] = jnp.zeros((halo, xs_ref.shape[1]), F32)

    @pl.when(blk > 0)
    def _():
        xs_ref[0:halo, :] = prev_ref[...].astype(F32)

    def stage(s, carry):
        rows = pl.multiple_of(s * c, c)
        xs_ref[pl.ds(halo + rows, c), :] = qkv_ref[pl.ds(rows, c), :].astype(F32)
        return carry

    lax.fori_loop(0, rblk // c, stage, 0)

    ri = _iota((c, c), 0)
    ci = _iota((c, c), 1)
    tri_incl = ri >= ci
    tri_strict = ri > ci
    eye = jnp.where(ri == ci, 1.0, 0.0).astype(F32)
    tri2 = jnp.where(_iota((c, 2 * c), 0) >= (_iota((c, 2 * c), 1) % c), 1.0, 0.0).astype(BF16)
    later = jnp.where(_iota((c, LANES), 0) > _iota((c, LANES), 1), 1.0, 0.0).astype(F32)

    nu = GDN_UNROLL
    heads = range(N_HEADS)
    hcs = [slice(h * hd, (h + 1) * hd) for h in heads]
    chains = [(u, h) for u in range(nu) for h in heads]

    def conv(win, col0):
        w = xs_ref[win, col0:col0 + hd]
        cw = conv_ref[:, col0:col0 + hd]
        y = w[halo:] * cw[GDN_CONV - 1:GDN_CONV]
        for j in range(1, GDN_CONV):
            y = y + pltpu.roll(w, j, axis=0)[halo:] * cw[GDN_CONV - 1 - j:GDN_CONV - j]
        return y * _sigmoid(y)

    def chunks(n, carry):
        base = n * (nu * c)
        r0 = [pl.multiple_of(base + u * c, c) for u in range(nu)]
        rows = [pl.ds(r0[u], c) for u in range(nu)]
        win = [pl.ds(r0[u], c + halo) for u in range(nu)]
        sm = [sm_ref[rows[u], :] for u in range(nu)]

        q = [conv(win[u], h * hd) for u, h in chains]
        k = [conv(win[u], N_HEADS * hd + h * hd) for u, h in chains]
        v = [conv(win[u], 2 * N_HEADS * hd + h * hd) for u, h in chains]
        q = [x * (lax.rsqrt(jnp.sum(x * x, axis=-1, keepdims=True) + L2_EPS) * (hd ** -0.5))
             for x in q]
        k = [x * lax.rsqrt(jnp.sum(x * x, axis=-1, keepdims=True) + L2_EPS) for x in k]
        beta = [_sigmoid(jnp.broadcast_to(sm[u][:, SM_DB + h:SM_DB + h + 1], (c, hd)))
                for u, h in chains]
        g = [-jnp.exp(hp_ref[0:1, hcs[h]])
             * _softplus(jnp.broadcast_to(sm[u][:, SM_DA + h:SM_DA + h + 1], (c, hd))
                         + hp_ref[1:2, hcs[h]])
             for u, h in chains]

        cum = []
        for x in g:
            xh, xl = _split(jnp.concatenate([x, x * later], axis=1))
            cum.append(_dot(tri2, jnp.concatenate([xh, xl], axis=0)))
        gam = [x[:, :hd] for x in cum]
        decay = [jnp.where(tri_incl, jnp.exp(x[:, hd:hd + c]), 0.0) for x in cum]
        eg = [jnp.exp(x) for x in gam]
        gl = [x[c - 1:c, :] for x in gam]

        kb = [x.astype(BF16) for x in k]
        kk = [_dot_nt(x, x) for x in kb]
        qk = [(_dot_nt(q[i].astype(BF16), kb[i]) * decay[i]).astype(BF16) for i in range(len(chains))]
        a = [jnp.where(tri_strict, beta[i][:, :c] * kk[i] * decay[i], 0.0) for i in range(len(chains))]
        tinv = _unit_lower_inverse(a, eye)
        sol = [_dot(tinv[i].astype(BF16),
                    jnp.concatenate([v[i] * beta[i], k[i] * (beta[i] * eg[i])], axis=1).astype(BF16))
               for i in range(len(chains))]
        lhs = [jnp.concatenate([sol[i][:, hd:], q[i] * eg[i]], axis=0).astype(BF16)
               for i in range(len(chains))]
        kd = [(k[i] * jnp.exp(gl[i] - gam[i])).astype(BF16) for i in range(len(chains))]
        cd = [jnp.exp(x) for x in gl]

        state = [st_ref[h] for h in heads]
        outs = []
        for u in range(nu):
            idx = [u * N_HEADS + h for h in heads]
            r = [_dot(lhs[i], state[h].astype(BF16)) for h, i in zip(heads, idx)]
            ub = [(sol[i][:, :hd] - r[h][:c]).astype(BF16) for h, i in zip(heads, idx)]
            outs.extend(r[h][c:] + _dot(qk[i], ub[h]) for h, i in zip(heads, idx))
            state = [state[h] * cd[i] + _dot_tn(kd[i], ub[h]) for h, i in zip(heads, idx)]
        for h in heads:
            st_ref[h] = state[h]
        for i, (u, h) in enumerate(chains):
            o = outs[i]
            on = o * lax.rsqrt(jnp.mean(o * o, axis=-1, keepdims=True) + RMS_EPS) * on_ref[...]
            zg = dz_ref[rows[u], hcs[h]].astype(F32)
            o_ref[rows[u], hcs[h]] = (on * (zg * _sigmoid(zg))).astype(BF16)
        return carry

    lax.fori_loop(0, rblk // (nu * c), chunks, 0)


def _gdn_call(proj3, small3, conv_w, hp, onorm, layer, *, rblk=512, halo=16):
    b, seq, _ = proj3.shape
    rblk = min(rblk, seq)
    w3 = 3 * N_HEADS * HEAD_DIM
    wz = N_HEADS * HEAD_DIM
    per = rblk // halo
    return pl.pallas_call(
        _gdn_kernel,
        out_shape=jax.ShapeDtypeStruct((b, seq, wz), BF16),
        grid=(b, seq // rblk),
        in_specs=[
            pl.BlockSpec((None, rblk, w3), lambda bi, i: (bi, i, OFF_DQKV // w3)),
            pl.BlockSpec((None, halo, w3),
                         lambda bi, i: (bi, jnp.maximum(i * per - 1, 0), OFF_DQKV // w3)),
            pl.BlockSpec((None, rblk, wz), lambda bi, i: (bi, i, OFF_DZ // wz)),
            pl.BlockSpec((None, rblk, LANES), lambda bi, i: (bi, i, 0)),
            pl.BlockSpec((None, GDN_CONV, w3), lambda bi, i: (layer, 0, 0)),
            pl.BlockSpec((None, 2, wz), lambda bi, i: (layer, 0, 0)),
            pl.BlockSpec((None, 1, HEAD_DIM), lambda bi, i: (layer, 0, 0)),
        ],
        out_specs=pl.BlockSpec((None, rblk, wz), lambda bi, i: (bi, i, 0)),
        scratch_shapes=[pltpu.VMEM((rblk + halo, w3), F32),
                        pltpu.VMEM((N_HEADS, HEAD_DIM, HEAD_DIM), F32)],
        compiler_params=pltpu.CompilerParams(
            dimension_semantics=("arbitrary", "arbitrary"), vmem_limit_bytes=VMEM_LIMIT),
        name="gdn",
    )(proj3, proj3, proj3, small3, conv_w, hp, onorm)


def _gla_kernel(q_ref, k_ref, v_ref, z_ref, sm_ref, w2_ref, b_ref, on_ref, o_ref, st_ref):
    rblk = q_ref.shape[0]
    hd = HEAD_DIM
    vd = GLA_VD
    c = CHUNK

    @pl.when(pl.program_id(1) == 0)
    def _():
        st_ref[...] = jnp.zeros_like(st_ref)

    tri_incl = _iota((c, c), 0) >= _iota((c, c), 1)
    tri2 = jnp.where(_iota((c, 2 * c), 0) >= (_iota((c, 2 * c), 1) % c), 1.0, 0.0).astype(BF16)
    nu = GLA_UNROLL
    heads = range(N_HEADS)
    hcs = [slice(h * hd, (h + 1) * hd) for h in heads]
    vcs = [slice(h * vd, (h + 1) * vd) for h in heads]
    chains = [(u, h) for u in range(nu) for h in heads]
    nc = len(chains)

    def chunks(n, carry):
        base = n * (nu * c)
        rows = [pl.ds(pl.multiple_of(base + u * c, c), c) for u in range(nu)]
        la = []
        for u in range(nu):
            x = _dot(sm_ref[rows[u], :].astype(BF16), w2_ref[...]) + b_ref[...]
            la.append(-_softplus(-x) * (1.0 / GLA_TAU))
        gc = []
        for u, h in chains:
            lh, ll = _split(la[u][:, hcs[h]])
            gc.append(_dot(tri2, jnp.concatenate([lh, ll], axis=0)))
        gl = [x[c - 1:c, :] for x in gc]
        q = [q_ref[rows[u], hcs[h]].astype(F32) * (hd ** -0.5) for u, h in chains]
        k = [k_ref[rows[u], hcs[h]].astype(F32) for u, h in chains]
        v = [v_ref[rows[u], vcs[h]] for u, h in chains]
        qd = [(q[i] * jnp.exp(gc[i])).astype(BF16) for i in range(nc)]
        att = []
        for i in range(nc):
            parts = []
            for j in range(c // SUB):
                rs = slice(j * SUB, (j + 1) * SUB)
                ref_g = gc[i][j * SUB:j * SUB + 1, :]
                qi = (q[i][rs] * jnp.exp(gc[i][rs] - ref_g)).astype(BF16)
                ki = (k[i] * jnp.exp(jnp.minimum(ref_g - gc[i], 60.0))).astype(BF16)
                parts.append(_dot_nt(qi, ki))
            att.append(jnp.where(tri_incl, jnp.concatenate(parts, axis=0), 0.0).astype(BF16))
        intra = [_dot(att[i], v[i]) for i in range(nc)]
        grow = [_dot_tn(v[i], (k[i] * jnp.exp(gl[i] - gc[i])).astype(BF16)) for i in range(nc)]
        keep = [jnp.exp(x) for x in gl]

        state = [st_ref[h] for h in heads]
        outs = []
        for u in range(nu):
            idx = [u * N_HEADS + h for h in heads]
            outs.extend(intra[i] + _dot_nt(qd[i], state[h].astype(BF16)) for h, i in zip(heads, idx))
            state = [state[h] * keep[i] + grow[i] for h, i in zip(heads, idx)]
        for h in heads:
            st_ref[h] = state[h]
        for i, (u, h) in enumerate(chains):
            o = outs[i]
            on = o * lax.rsqrt(jnp.mean(o * o, axis=-1, keepdims=True) + RMS_EPS) * on_ref[...]
            zg = z_ref[rows[u], vcs[h]].astype(F32)
            o_ref[rows[u], vcs[h]] = (on * (zg * _sigmoid(zg))).astype(BF16)
        return carry

    lax.fori_loop(0, rblk // (nu * c), chunks, 0)


def _gla_call(proj3, small3, w2p, gla_b, onorm, layer, *, rblk=512):
    b, seq, _ = proj3.shape
    rblk = min(rblk, seq)
    wk = N_HEADS * HEAD_DIM
    wv = N_HEADS * GLA_VD
    return pl.pallas_call(
        _gla_kernel,
        out_shape=jax.ShapeDtypeStruct((b, seq, wv), BF16),
        grid=(b, seq // rblk),
        in_specs=[
            pl.BlockSpec((None, rblk, wk), lambda bi, i: (bi, i, OFF_LQ // wk)),
            pl.BlockSpec((None, rblk, wk), lambda bi, i: (bi, i, OFF_LK // wk)),
            pl.BlockSpec((None, rblk, wv), lambda bi, i: (bi, i, OFF_LV // wv)),
            pl.BlockSpec((None, rblk, wv), lambda bi, i: (bi, i, OFF_LZ // wv)),
            pl.BlockSpec((None, rblk, LANES), lambda bi, i: (bi, i, 0)),
            pl.BlockSpec((None, LANES, wk), lambda bi, i: (layer, 0, 0)),
            pl.BlockSpec((None, 1, wk), lambda bi, i: (layer, 0, 0)),
            pl.BlockSpec((None, 1, GLA_VD), lambda bi, i: (layer, 0, 0)),
        ],
        out_specs=pl.BlockSpec((None, rblk, wv), lambda bi, i: (bi, i, 0)),
        scratch_shapes=[pltpu.VMEM((N_HEADS, GLA_VD, HEAD_DIM), F32)],
        compiler_params=pltpu.CompilerParams(
            dimension_semantics=("arbitrary", "arbitrary"), vmem_limit_bytes=VMEM_LIMIT),
        name="gla",
    )(proj3, proj3, proj3, proj3, small3, w2p, gla_b, onorm)


def _out_kernel(ysb_ref, ygdn_ref, ygla_ref, mg_ref, mb_ref, psb_ref, pgdn_ref, pgla_ref,
                wout_ref, x_ref, mod_ref, o_ref):
    d = x_ref.shape[1]
    gates = _sigmoid(mg_ref[...].astype(F32) + mb_ref[...])
    merged = gates[:, 0:d] * _dot(ysb_ref[...], psb_ref[...])
    merged = merged + gates[:, d:2 * d] * _dot(ygdn_ref[...], pgdn_ref[...])
    merged = merged + gates[:, 2 * d:3 * d] * _dot(ygla_ref[...], pgla_ref[...])
    o_ref[...] = x_ref[...] + mod_ref[:, 2 * d:3 * d] * _dot(merged.astype(BF16), wout_ref[...])


def _out_call(ysb, ygdn, ygla, proj, merge_b, psb, pgdn, pgla, wout, x2, mod, layer, seq, *, tm):
    t, d = x2.shape
    tm = min(tm, seq)
    rows_per_seq = seq // tm
    wsb, wgdn, wgla = ysb.shape[1], ygdn.shape[1], ygla.shape[1]

    def wspec(k):
        return pl.BlockSpec((None, k, d), lambda i: (layer, 0, 0))

    return pl.pallas_call(
        _out_kernel,
        out_shape=jax.ShapeDtypeStruct((t, d), F32),
        grid=(t // tm,),
        in_specs=[
            pl.BlockSpec((tm, wsb), lambda i: (i, 0)),
            pl.BlockSpec((tm, wgdn), lambda i: (i, 0)),
            pl.BlockSpec((tm, wgla), lambda i: (i, 0)),
            pl.BlockSpec((tm, N_BRANCH * d), lambda i: (i, OFF_MG // (N_BRANCH * d))),
            pl.BlockSpec((None, 1, N_BRANCH * d), lambda i: (layer, 0, 0)),
            wspec(wsb), wspec(wgdn), wspec(wgla), wspec(d),
            pl.BlockSpec((tm, d), lambda i: (i, 0)),
            pl.BlockSpec((None, None, 1, 3 * d), lambda i: (layer, i // rows_per_seq, 0, 0)),
        ],
        out_specs=pl.BlockSpec((tm, d), lambda i: (i, 0)),
        compiler_params=pltpu.CompilerParams(
            dimension_semantics=("arbitrary",), vmem_limit_bytes=VMEM_LIMIT),
        name="merge_out",
    )(ysb, ygdn, ygla, proj, merge_b, psb, pgdn, pgla, wout, x2, mod)


def _reorder_w_in(w_in):
    hw = N_HEADS * HEAD_DIM
    sb = w_in[..., 0:4 * hw]
    dqkv = w_in[..., 4 * hw:7 * hw]
    dz = w_in[..., 7 * hw:8 * hw]
    o = 8 * hw
    db = w_in[..., o:o + N_HEADS]
    da = w_in[..., o + N_HEADS:o + 2 * N_HEADS]
    o += 2 * N_HEADS
    lqk = w_in[..., o:o + 2 * hw]
    lvz = w_in[..., o + 2 * hw:o + 2 * hw + 2 * N_HEADS * GLA_VD]
    o += 2 * hw + 2 * N_HEADS * GLA_VD
    lr = w_in[..., o:o + GLA_RANK]
    mg = w_in[..., o + GLA_RANK:]
    main = jnp.concatenate([mg, dqkv, dz, sb, lqk, lvz], axis=-1).astype(BF16)
    pad = jnp.zeros(w_in.shape[:-1] + (LANES - GLA_RANK - 2 * N_HEADS,), w_in.dtype)
    small = jnp.concatenate([lr, db, da, pad], axis=-1).astype(BF16)
    return main, small


def kernel(x, c, ada_w, ada_b, norm_g, w_in, sb_qnorm, sb_knorm, gdn_conv, gdn_a_log, gdn_dt_bias,
           gdn_onorm, gla_w2, gla_b, gla_onorm, merge_b, proj_sb, proj_gdn, proj_gla, w_out):
    b, seq, d = x.shape
    n_l = w_in.shape[0]
    t = b * seq

    w_main, w_small = _reorder_w_in(w_in)
    assert w_main.shape[-1] == N_MAIN
    mod = _ada_call(c, ada_w, ada_b).reshape(n_l, b, 1, 3 * d)
    hp = jnp.stack([jnp.repeat(gdn_a_log, HEAD_DIM, axis=-1),
                    jnp.repeat(gdn_dt_bias, HEAD_DIM, axis=-1)], axis=1)
    w2p = jnp.pad(gla_w2, ((0, 0), (SM_LR, LANES - SM_LR - GLA_RANK), (0, 0))).astype(BF16)
    r3 = lambda a: a.reshape(n_l, 1, a.shape[-1])
    psb, pgdn, pgla, wout = (a.astype(BF16) for a in (proj_sb, proj_gdn, proj_gla, w_out))

    x2 = x.reshape(t, d)
    for layer in range(n_l):
        proj, small = _in_call(x2, mod, r3(norm_g), w_main, w_small, layer, seq, tm=1024, tn=2048)
        proj3 = proj.reshape(b, seq, N_MAIN)
        small3 = small.reshape(b, seq, LANES)
        ysb = _sb_call(proj3, r3(sb_qnorm), r3(sb_knorm), layer)
        ygdn = _gdn_call(proj3, small3, gdn_conv, hp, r3(gdn_onorm), layer)
        ygla = _gla_call(proj3, small3, w2p, r3(gla_b), r3(gla_onorm), layer)
        x2 = _out_call(ysb.reshape(t, -1), ygdn.reshape(t, -1), ygla.reshape(t, -1), proj,
                       r3(merge_b), psb, pgdn, pgla, wout, x2, mod, layer, seq, tm=512)
    return x2.reshape(b, seq, d)
```

```python
import functools

import jax
import jax.numpy as jnp
from jax import lax
from jax.experimental import pallas as pl
from jax.experimental.pallas import tpu as pltpu

F32 = jnp.float32
BF16 = jnp.bfloat16

D_MODEL = 1024
N_HEADS = 4
HEAD_DIM = 128
GLA_VD = 256
GDN_CONV = 4
GLA_RANK = 16
GLA_TAU = 16.0
N_BRANCH = 3
RMS_EPS = 1e-6
L2_EPS = 1e-6
CHUNK = 64
GDN_UNROLL = 4
GLA_UNROLL = 2
SUB = 16
LANES = 128
LOG2E = 1.4426950408889634
SB_DEAD = -160.0
INV_BASE = 8
VMEM_LIMIT = 56 * 1024 * 1024

OFF_MG = 0
OFF_DQKV = 3072
OFF_DZ = 4608
OFF_SQ, OFF_SK, OFF_SV, OFF_SZ = 5120, 5632, 6144, 6656
OFF_LQ, OFF_LK, OFF_LV, OFF_LZ = 7168, 7680, 8192, 9216
N_MAIN = 10240
SM_LR, SM_DB, SM_DA = 0, 16, 20


def _dot(a, b):
    return jnp.dot(a, b, preferred_element_type=F32)


def _dot_nt(a, b):
    return lax.dot_general(a, b, (((1,), (1,)), ((), ())), preferred_element_type=F32)


def _dot_tn(a, b):
    return lax.dot_general(a, b, (((0,), (0,)), ((), ())), preferred_element_type=F32)


def _split(x):
    hi = x.astype(BF16)
    lo = (x - hi.astype(F32)).astype(BF16)
    return hi, lo


def _sigmoid(x):
    return 1.0 / (1.0 + jnp.exp(-x))


def _softplus(x):
    return jnp.maximum(x, 0.0) + jnp.log(1.0 + jnp.exp(-jnp.abs(x)))


def _iota(shape, axis):
    return lax.broadcasted_iota(jnp.int32, shape, axis)


def _ada_kernel(c_ref, w_ref, b_ref, o_ref):
    c = c_ref[...]
    ca = c * _sigmoid(c)
    o_ref[...] = jnp.dot(ca, w_ref[...], preferred_element_type=F32,
                         precision=lax.Precision.HIGHEST) + b_ref[...]


def _ada_call(c, ada_w, ada_b):
    n_l, d, d3 = ada_w.shape
    b = c.shape[0]
    tn = 768
    return pl.pallas_call(
        _ada_kernel,
        out_shape=jax.ShapeDtypeStruct((n_l, b, d3), F32),
        grid=(n_l, d3 // tn),
        in_specs=[
            pl.BlockSpec((b, d), lambda l, j: (0, 0)),
            pl.BlockSpec((None, d, tn), lambda l, j: (l, 0, j)),
            pl.BlockSpec((None, 1, tn), lambda l, j: (l, 0, j)),
        ],
        out_specs=pl.BlockSpec((None, b, tn), lambda l, j: (l, 0, j)),
        compiler_params=pltpu.CompilerParams(
            dimension_semantics=("arbitrary", "arbitrary"), vmem_limit_bytes=VMEM_LIMIT),
        name="adaln_mod",
    )(c, ada_w, ada_b.reshape(n_l, 1, d3))


def _in_kernel(x_ref, mod_ref, g_ref, w_ref, ws_ref, out_ref, small_ref, h_ref, *, slab):
    tm, d = x_ref.shape

    @pl.when(pl.program_id(1) == 0)
    def _():
        shift = mod_ref[:, 0:d]
        scale1 = 1.0 + mod_ref[:, d:2 * d]
        gain = g_ref[...]

        def body(s, carry):
            rows = pl.ds(pl.multiple_of(s * slab, slab), slab)
            x = x_ref[rows, :]
            y = x * lax.rsqrt(jnp.mean(x * x, axis=-1, keepdims=True) + RMS_EPS) * gain
            h_ref[rows, :] = (y * scale1 + shift).astype(BF16)
            return carry

        lax.fori_loop(0, tm // slab, body, 0)
        small_ref[...] = _dot(h_ref[...], ws_ref[...])

    out_ref[...] = _dot(h_ref[...], w_ref[...]).astype(BF16)


def _in_call(x2, mod, norm_g, w_main, w_small, layer, seq, *, tm, tn):
    t, d = x2.shape
    n_main = w_main.shape[-1]
    tm = min(tm, seq)
    rows_per_seq = seq // tm
    kern = functools.partial(_in_kernel, slab=min(128, tm))
    return pl.pallas_call(
        kern,
        out_shape=(jax.ShapeDtypeStruct((t, n_main), BF16),
                   jax.ShapeDtypeStruct((t, LANES), F32)),
        grid=(t // tm, n_main // tn),
        in_specs=[
            pl.BlockSpec((tm, d), lambda i, j: (i, 0)),
            pl.BlockSpec((None, None, 1, 3 * d), lambda i, j: (layer, i // rows_per_seq, 0, 0)),
            pl.BlockSpec((None, 1, d), lambda i, j: (layer, 0, 0)),
            pl.BlockSpec((None, d, tn), lambda i, j: (layer, 0, j)),
            pl.BlockSpec((None, d, LANES), lambda i, j: (layer, 0, 0)),
        ],
        out_specs=(pl.BlockSpec((tm, tn), lambda i, j: (i, j)),
                   pl.BlockSpec((tm, LANES), lambda i, j: (i, 0))),
        scratch_shapes=[pltpu.VMEM((tm, d), BF16)],
        compiler_params=pltpu.CompilerParams(
            dimension_semantics=("arbitrary", "arbitrary"), vmem_limit_bytes=VMEM_LIMIT),
        name="in_proj",
    )(x2, mod, norm_g, w_main, w_small)


def _sb_kernel(q_ref, k_ref, v_ref, z_ref, qw_ref, kw_ref, o_ref, kn_ref, qn_ref, acc_ref, run_ref,
               *, tq):
    seq = q_ref.shape[0]
    hd = HEAD_DIM
    scale = hd ** -0.5
    heads = range(N_HEADS)
    hcs = [slice(h * hd, (h + 1) * hd) for h in heads]

    def rms(x, w):
        return x * lax.rsqrt(jnp.mean(x * x, axis=-1, keepdims=True) + RMS_EPS) * w

    def knorm(s, carry):
        rows = pl.ds(pl.multiple_of(s * tq, tq), tq)
        for h in heads:
            kn_ref[rows, hcs[h]] = rms(k_ref[rows, hcs[h]].astype(F32), kw_ref[...]).astype(BF16)
        return carry

    lax.fori_loop(0, seq // tq, knorm, 0)

    later = jnp.where(_iota((tq, tq), 0) > _iota((tq, tq), 1), 1.0, 0.0).astype(BF16)
    sign = jnp.int32(-2 ** 31)

    def step(k0, qpos):
        keys = pl.ds(k0, tq)
        z = [_dot_nt(qn_ref[h], kn_ref[keys, hcs[h]]) for h in heads]
        t = [jnp.log2(1.0 + jnp.exp2(pltpu.bitcast(pltpu.bitcast(x, jnp.int32) | sign, F32))) for x in z]
        lb = [jnp.minimum(x, 0.0) - y for x, y in zip(z, t)]
        lg = [x - y for x, y in zip(lb, z)]
        if qpos is not None:
            mask = (k0 + _iota((tq, tq), 1)) < qpos
            lg = [jnp.where(mask, x, 0.0) for x in lg]
        tail = [_dot(x.astype(BF16), later) for x in lg]
        ws = []
        for h in heads:
            run = run_ref[h]
            w = jnp.exp2(lb[h] + tail[h] + run)
            if qpos is not None:
                w = jnp.where(mask, w, 0.0)
            ws.append(w.astype(BF16))
            run_ref[h] = run + jnp.sum(lg[h], axis=1, keepdims=True)
        for h in heads:
            acc_ref[h] += _dot(ws[h], v_ref[keys, hcs[h]])

    def q_block(i, carry):
        r0 = pl.multiple_of(i * tq, tq)
        rows = pl.ds(r0, tq)
        for h in heads:
            qn_ref[h] = (rms(q_ref[rows, hcs[h]].astype(F32), qw_ref[...]) * (scale * LOG2E)).astype(BF16)
            acc_ref[h] = jnp.zeros((tq, hd), F32)
            run_ref[h] = jnp.zeros((tq, 1), F32)
        step(r0, r0 + _iota((tq, tq), 0))

        def alive():
            return jnp.max(run_ref[...]) > SB_DEAD

        def below(carry):
            n, _ = carry
            step(pl.multiple_of((i - 1 - n) * tq, tq), None)
            return n + 1, alive()

        lax.while_loop(lambda carry: (carry[0] < i) & carry[1], below, (0, alive()))
        for h in heads:
            zg = z_ref[rows, hcs[h]].astype(F32)
            o_ref[rows, hcs[h]] = (acc_ref[h] * (zg * _sigmoid(zg))).astype(BF16)
        return carry

    lax.fori_loop(0, seq // tq, q_block, 0)


def _sb_call(proj3, qw, kw, layer, *, tq=256):
    b, seq, _ = proj3.shape
    hd = HEAD_DIM
    w = N_HEADS * hd

    def col(off):
        return pl.BlockSpec((None, seq, w), lambda bi: (bi, 0, off // w))

    wspec = pl.BlockSpec((None, 1, hd), lambda bi: (layer, 0, 0))
    return pl.pallas_call(
        functools.partial(_sb_kernel, tq=tq),
        out_shape=jax.ShapeDtypeStruct((b, seq, w), BF16),
        grid=(b,),
        in_specs=[col(OFF_SQ), col(OFF_SK), col(OFF_SV), col(OFF_SZ), wspec, wspec],
        out_specs=pl.BlockSpec((None, seq, w), lambda bi: (bi, 0, 0)),
        scratch_shapes=[pltpu.VMEM((seq, w), BF16),
                        pltpu.VMEM((N_HEADS, tq, hd), BF16),
                        pltpu.VMEM((N_HEADS, tq, hd), F32),
                        pltpu.VMEM((N_HEADS, tq, 1), F32)],
        compiler_params=pltpu.CompilerParams(
            dimension_semantics=("arbitrary",), vmem_limit_bytes=VMEM_LIMIT),
        name="sb_attn",
    )(proj3, proj3, proj3, proj3, qw, kw)


def _unit_lower_inverse(a_list, ri, ci):
    size = ri.shape[0]
    same = lambda w: (ri // w) == (ci // w)
    eye = jnp.where(ri == ci, 1.0, 0.0).astype(F32)
    d = [jnp.where(same(INV_BASE), a, 0.0) for a in a_list]
    x = [eye - y for y in d]
    p = d
    w = 2
    while w < INV_BASE:
        pb = [y.astype(BF16) for y in p]
        p = [_dot(y, y) for y in pb]
        x = [y + _dot(y.astype(BF16), z.astype(BF16)) for y, z in zip(x, p)]
        w *= 2
    w = INV_BASE
    while w < size:
        below = same(2 * w) & jnp.logical_not(same(w))
        xb = [y.astype(BF16) for y in x]
        lx = [_dot(jnp.where(below, a, 0.0).astype(BF16), y) for a, y in zip(a_list, xb)]
        x = [y - _dot(yb, z.astype(BF16)) for y, yb, z in zip(x, xb, lx)]
        w *= 2
    return x


def _gdn_kernel(qkv_ref, prev_ref, dz_ref, sm_ref, conv_ref, hp_ref, on_ref, o_ref,
                xs_ref, st_ref):
    rblk = qkv_ref.shape[0]
    hd = HEAD_DIM
    c = CHUNK
    halo = prev_ref.shape[0]
    blk = pl.program_id(1)

    @pl.when(blk == 0)
    def _():
        st_ref[...] = jnp.zeros_like(st_ref)
        xs_ref[0:halo, :] = jnp.zeros((halo, xs_ref.shape[1]), F32)

    @pl.when(blk > 0)
    def _():
        xs_ref[0:halo, :] = prev_ref[...].astype(F32)

    def stage(s, carry):
        rows = pl.multiple_of(s * c, c)
        xs_ref[pl.ds(halo + rows, c), :] = qkv_ref[pl.ds(rows, c), :].astype(F32)
        return carry

    lax.fori_loop(0, rblk // c, stage, 0)

    ri = _iota((c, c), 0)
    ci = _iota((c, c), 1)
    tri_incl = ri >= ci
    tri_strict = ri > ci
    tri2 = jnp.where(_iota((c, 2 * c), 0) >= (_iota((c, 2 * c), 1) % c), 1.0, 0.0).astype(BF16)
    later = jnp.where(_iota((c, LANES), 0) > _iota((c, LANES), 1), 1.0, 0.0).astype(F32)

    nu = GDN_UNROLL
    heads = range(N_HEADS)
    hcs = [slice(h * hd, (h + 1) * hd) for h in heads]
    chains = [(u, h) for u in range(nu) for h in heads]

    def conv(win, col0):
        w = xs_ref[win, col0:col0 + hd]
        cw = conv_ref[:, col0:col0 + hd]
        y = w[halo:] * cw[GDN_CONV - 1:GDN_CONV]
        for j in range(1, GDN_CONV):
            y = y + pltpu.roll(w, j, axis=0)[halo:] * cw[GDN_CONV - 1 - j:GDN_CONV - j]
        return y * _sigmoid(y)

    def chunks(n, carry):
        base = n * (nu * c)
        r0 = [pl.multiple_of(base + u * c, c) for u in range(nu)]
        rows = [pl.ds(r0[u], c) for u in range(nu)]
        win = [pl.ds(r0[u], c + halo) for u in range(nu)]
        sm = [sm_ref[rows[u], :] for u in range(nu)]

        q = [conv(win[u], h * hd) for u, h in chains]
        k = [conv(win[u], N_HEADS * hd + h * hd) for u, h in chains]
        v = [conv(win[u], 2 * N_HEADS * hd + h * hd) for u, h in chains]
        q = [x * (lax.rsqrt(jnp.sum(x * x, axis=-1, keepdims=True) + L2_EPS) * (hd ** -0.5))
             for x in q]
        k = [x * lax.rsqrt(jnp.sum(x * x, axis=-1, keepdims=True) + L2_EPS) for x in k]
        beta = [_sigmoid(jnp.broadcast_to(sm[u][:, SM_DB + h:SM_DB + h + 1], (c, hd)))
                for u, h in chains]
        g = [-jnp.exp(hp_ref[0:1, hcs[h]])
             * _softplus(jnp.broadcast_to(sm[u][:, SM_DA + h:SM_DA + h + 1], (c, hd))
                         + hp_ref[1:2, hcs[h]])
             for u, h in chains]

        cum = []
        for x in g:
            xh, xl = _split(jnp.concatenate([x, x * later], axis=1))
            cum.append(_dot(tri2, jnp.concatenate([xh, xl], axis=0)))
        gam = [x[:, :hd] for x in cum]
        decay = [jnp.where(tri_incl, jnp.exp(x[:, hd:hd + c]), 0.0) for x in cum]
        eg = [jnp.exp(x) for x in gam]
        gl = [x[c - 1:c, :] for x in gam]

        kb = [x.astype(BF16) for x in k]
        kk = [_dot_nt(x, x) for x in kb]
        qk = [(_dot_nt(q[i].astype(BF16), kb[i]) * decay[i]).astype(BF16) for i in range(len(chains))]
        a = [jnp.where(tri_strict, beta[i][:, :c] * kk[i] * decay[i], 0.0) for i in range(len(chains))]
        tinv = _unit_lower_inverse(a, ri, ci)
        sol = [_dot(tinv[i].astype(BF16),
                    jnp.concatenate([v[i] * beta[i], k[i] * (beta[i] * eg[i])], axis=1).astype(BF16))
               for i in range(len(chains))]
        lhs = [jnp.concatenate([sol[i][:, hd:], q[i] * eg[i]], axis=0).astype(BF16)
               for i in range(len(chains))]
        kd = [(k[i] * jnp.exp(gl[i] - gam[i])).astype(BF16) for i in range(len(chains))]
        cd = [jnp.exp(x) for x in gl]

        state = [st_ref[h] for h in heads]
        outs = []
        for u in range(nu):
            idx = [u * N_HEADS + h for h in heads]
            r = [_dot(lhs[i], state[h].astype(BF16)) for h, i in zip(heads, idx)]
            ub = [(sol[i][:, :hd] - r[h][:c]).astype(BF16) for h, i in zip(heads, idx)]
            outs.extend(r[h][c:] + _dot(qk[i], ub[h]) for h, i in zip(heads, idx))
            state = [state[h] * cd[i] + _dot_tn(kd[i], ub[h]) for h, i in zip(heads, idx)]
        for h in heads:
            st_ref[h] = state[h]
        for i, (u, h) in enumerate(chains):
            o = outs[i]
            on = o * lax.rsqrt(jnp.mean(o * o, axis=-1, keepdims=True) + RMS_EPS) * on_ref[...]
            zg = dz_ref[rows[u], hcs[h]].astype(F32)
            o_ref[rows[u], hcs[h]] = (on * (zg * _sigmoid(zg))).astype(BF16)
        return carry

    lax.fori_loop(0, rblk // (nu * c), chunks, 0)


def _gdn_call(proj3, small3, conv_w, hp, onorm, layer, *, rblk=512, halo=16):
    b, seq, _ = proj3.shape
    rblk = min(rblk, seq)
    w3 = 3 * N_HEADS * HEAD_DIM
    wz = N_HEADS * HEAD_DIM
    per = rblk // halo
    return pl.pallas_call(
        _gdn_kernel,
        out_shape=jax.ShapeDtypeStruct((b, seq, wz), BF16),
        grid=(b, seq // rblk),
        in_specs=[
            pl.BlockSpec((None, rblk, w3), lambda bi, i: (bi, i, OFF_DQKV // w3)),
            pl.BlockSpec((None, halo, w3),
                         lambda bi, i: (bi, jnp.maximum(i * per - 1, 0), OFF_DQKV // w3)),
            pl.BlockSpec((None, rblk, wz), lambda bi, i: (bi, i, OFF_DZ // wz)),
            pl.BlockSpec((None, rblk, LANES), lambda bi, i: (bi, i, 0)),
            pl.BlockSpec((None, GDN_CONV, w3), lambda bi, i: (layer, 0, 0)),
            pl.BlockSpec((None, 2, wz), lambda bi, i: (layer, 0, 0)),
            pl.BlockSpec((None, 1, HEAD_DIM), lambda bi, i: (layer, 0, 0)),
        ],
        out_specs=pl.BlockSpec((None, rblk, wz), lambda bi, i: (bi, i, 0)),
        scratch_shapes=[pltpu.VMEM((rblk + halo, w3), F32),
                        pltpu.VMEM((N_HEADS, HEAD_DIM, HEAD_DIM), F32)],
        compiler_params=pltpu.CompilerParams(
            dimension_semantics=("arbitrary", "arbitrary"), vmem_limit_bytes=VMEM_LIMIT),
        name="gdn",
    )(proj3, proj3, proj3, small3, conv_w, hp, onorm)


def _gla_kernel(q_ref, k_ref, v_ref, z_ref, sm_ref, w2_ref, b_ref, on_ref, o_ref, st_ref):
    rblk = q_ref.shape[0]
    hd = HEAD_DIM
    vd = GLA_VD
    c = CHUNK

    @pl.when(pl.program_id(1) == 0)
    def _():
        st_ref[...] = jnp.zeros_like(st_ref)

    tri_incl = _iota((c, c), 0) >= _iota((c, c), 1)
    tri2 = jnp.where(_iota((c, 2 * c), 0) >= (_iota((c, 2 * c), 1) % c), 1.0, 0.0).astype(BF16)
    nu = GLA_UNROLL
    heads = range(N_HEADS)
    hcs = [slice(h * hd, (h + 1) * hd) for h in heads]
    vcs = [slice(h * vd, (h + 1) * vd) for h in heads]
    chains = [(u, h) for u in range(nu) for h in heads]
    nc = len(chains)

    def chunks(n, carry):
        base = n * (nu * c)
        rows = [pl.ds(pl.multiple_of(base + u * c, c), c) for u in range(nu)]
        la = []
        for u in range(nu):
            x = _dot(sm_ref[rows[u], :].astype(BF16), w2_ref[...]) + b_ref[...]
            la.append(-_softplus(-x) * (1.0 / GLA_TAU))
        gc = []
        for u, h in chains:
            lh, ll = _split(la[u][:, hcs[h]])
            gc.append(_dot(tri2, jnp.concatenate([lh, ll], axis=0)))
        gl = [x[c - 1:c, :] for x in gc]
        q = [q_ref[rows[u], hcs[h]].astype(F32) * (hd ** -0.5) for u, h in chains]
        k = [k_ref[rows[u], hcs[h]].astype(F32) for u, h in chains]
        v = [v_ref[rows[u], vcs[h]] for u, h in chains]
        qd = [(q[i] * jnp.exp(gc[i])).astype(BF16) for i in range(nc)]
        att = []
        for i in range(nc):
            parts = []
            for j in range(c // SUB):
                rs = slice(j * SUB, (j + 1) * SUB)
                ref_g = gc[i][j * SUB:j * SUB + 1, :]
                qi = (q[i][rs] * jnp.exp(gc[i][rs] - ref_g)).astype(BF16)
                ki = (k[i] * jnp.exp(jnp.minimum(ref_g - gc[i], 60.0))).astype(BF16)
                parts.append(_dot_nt(qi, ki))
            att.append(jnp.where(tri_incl, jnp.concatenate(parts, axis=0), 0.0).astype(BF16))
        intra = [_dot(att[i], v[i]) for i in range(nc)]
        grow = [_dot_tn(v[i], (k[i] * jnp.exp(gl[i] - gc[i])).astype(BF16)) for i in range(nc)]
        keep = [jnp.exp(x) for x in gl]

        state = [st_ref[h] for h in heads]
        outs = []
        for u in range(nu):
            idx = [u * N_HEADS + h for h in heads]
            outs.extend(intra[i] + _dot_nt(qd[i], state[h].astype(BF16)) for h, i in zip(heads, idx))
            state = [state[h] * keep[i] + grow[i] for h, i in zip(heads, idx)]
        for h in heads:
            st_ref[h] = state[h]
        for i, (u, h) in enumerate(chains):
            o = outs[i]
            on = o * lax.rsqrt(jnp.mean(o * o, axis=-1, keepdims=True) + RMS_EPS) * on_ref[...]
            zg = z_ref[rows[u], vcs[h]].astype(F32)
            o_ref[rows[u], vcs[h]] = (on * (zg * _sigmoid(zg))).astype(BF16)
        return carry

    lax.fori_loop(0, rblk // (nu * c), chunks, 0)


def _gla_call(proj3, small3, w2p, gla_b, onorm, layer, *, rblk=512):
    b, seq, _ = proj3.shape
    rblk = min(rblk, seq)
    wk = N_HEADS * HEAD_DIM
    wv = N_HEADS * GLA_VD
    return pl.pallas_call(
        _gla_kernel,
        out_shape=jax.ShapeDtypeStruct((b, seq, wv), BF16),
        grid=(b, seq // rblk),
        in_specs=[
            pl.BlockSpec((None, rblk, wk), lambda bi, i: (bi, i, OFF_LQ // wk)),
            pl.BlockSpec((None, rblk, wk), lambda bi, i: (bi, i, OFF_LK // wk)),
            pl.BlockSpec((None, rblk, wv), lambda bi, i: (bi, i, OFF_LV // wv)),
            pl.BlockSpec((None, rblk, wv), lambda bi, i: (bi, i, OFF_LZ // wv)),
            pl.BlockSpec((None, rblk, LANES), lambda bi, i: (bi, i, 0)),
            pl.BlockSpec((None, LANES, wk), lambda bi, i: (layer, 0, 0)),
            pl.BlockSpec((None, 1, wk), lambda bi, i: (layer, 0, 0)),
            pl.BlockSpec((None, 1, GLA_VD), lambda bi, i: (layer, 0, 0)),
        ],
        out_specs=pl.BlockSpec((None, rblk, wv), lambda bi, i: (bi, i, 0)),
        scratch_shapes=[pltpu.VMEM((N_HEADS, GLA_VD, HEAD_DIM), F32)],
        compiler_params=pltpu.CompilerParams(
            dimension_semantics=("arbitrary", "arbitrary"), vmem_limit_bytes=VMEM_LIMIT),
        name="gla",
    )(proj3, proj3, proj3, proj3, small3, w2p, gla_b, onorm)


def _out_kernel(ysb_ref, ygdn_ref, ygla_ref, mg_ref, mb_ref, psb_ref, pgdn_ref, pgla_ref,
                wout_ref, x_ref, mod_ref, o_ref):
    d = x_ref.shape[1]
    gates = _sigmoid(mg_ref[...].astype(F32) + mb_ref[...])
    merged = gates[:, 0:d] * _dot(ysb_ref[...], psb_ref[...])
    merged = merged + gates[:, d:2 * d] * _dot(ygdn_ref[...], pgdn_ref[...])
    merged = merged + gates[:, 2 * d:3 * d] * _dot(ygla_ref[...], pgla_ref[...])
    o_ref[...] = x_ref[...] + mod_ref[:, 2 * d:3 * d] * _dot(merged.astype(BF16), wout_ref[...])


def _out_call(ysb, ygdn, ygla, proj, merge_b, psb, pgdn, pgla, wout, x2, mod, layer, seq, *, tm):
    t, d = x2.shape
    tm = min(tm, seq)
    rows_per_seq = seq // tm
    wsb, wgdn, wgla = ysb.shape[1], ygdn.shape[1], ygla.shape[1]

    def wspec(k):
        return pl.BlockSpec((None, k, d), lambda i: (layer, 0, 0))

    return pl.pallas_call(
        _out_kernel,
        out_shape=jax.ShapeDtypeStruct((t, d), F32),
        grid=(t // tm,),
        in_specs=[
            pl.BlockSpec((tm, wsb), lambda i: (i, 0)),
            pl.BlockSpec((tm, wgdn), lambda i: (i, 0)),
            pl.BlockSpec((tm, wgla), lambda i: (i, 0)),
            pl.BlockSpec((tm, N_BRANCH * d), lambda i: (i, OFF_MG // (N_BRANCH * d))),
            pl.BlockSpec((None, 1, N_BRANCH * d), lambda i: (layer, 0, 0)),
            wspec(wsb), wspec(wgdn), wspec(wgla), wspec(d),
            pl.BlockSpec((tm, d), lambda i: (i, 0)),
            pl.BlockSpec((None, None, 1, 3 * d), lambda i: (layer, i // rows_per_seq, 0, 0)),
        ],
        out_specs=pl.BlockSpec((tm, d), lambda i: (i, 0)),
        compiler_params=pltpu.CompilerParams(
            dimension_semantics=("arbitrary",), vmem_limit_bytes=VMEM_LIMIT),
        name="merge_out",
    )(ysb, ygdn, ygla, proj, merge_b, psb, pgdn, pgla, wout, x2, mod)


def _reorder_w_in(w_in):
    hw = N_HEADS * HEAD_DIM
    sb = w_in[..., 0:4 * hw]
    dqkv = w_in[..., 4 * hw:7 * hw]
    dz = w_in[..., 7 * hw:8 * hw]
    o = 8 * hw
    db = w_in[..., o:o + N_HEADS]
    da = w_in[..., o + N_HEADS:o + 2 * N_HEADS]
    o += 2 * N_HEADS
    lqk = w_in[..., o:o + 2 * hw]
    lvz = w_in[..., o + 2 * hw:o + 2 * hw + 2 * N_HEADS * GLA_VD]
    o += 2 * hw + 2 * N_HEADS * GLA_VD
    lr = w_in[..., o:o + GLA_RANK]
    mg = w_in[..., o + GLA_RANK:]
    main = jnp.concatenate([mg, dqkv, dz, sb, lqk, lvz], axis=-1).astype(BF16)
    pad = jnp.zeros(w_in.shape[:-1] + (LANES - GLA_RANK - 2 * N_HEADS,), w_in.dtype)
    small = jnp.concatenate([lr, db, da, pad], axis=-1).astype(BF16)
    return main, small


def kernel(x, c, ada_w, ada_b, norm_g, w_in, sb_qnorm, sb_knorm, gdn_conv, gdn_a_log, gdn_dt_bias,
           gdn_onorm, gla_w2, gla_b, gla_onorm, merge_b, proj_sb, proj_gdn, proj_gla, w_out):
    b, seq, d = x.shape
    n_l = w_in.shape[0]
    t = b * seq

    w_main, w_small = _reorder_w_in(w_in)
    assert w_main.shape[-1] == N_MAIN
    mod = _ada_call(c, ada_w, ada_b).reshape(n_l, b, 1, 3 * d)
    hp = jnp.stack([jnp.repeat(gdn_a_log, HEAD_DIM, axis=-1),
                    jnp.repeat(gdn_dt_bias, HEAD_DIM, axis=-1)], axis=1)
    w2p = jnp.pad(gla_w2, ((0, 0), (SM_LR, LANES - SM_LR - GLA_RANK), (0, 0))).astype(BF16)
    r3 = lambda a: a.reshape(n_l, 1, a.shape[-1])
    psb, pgdn, pgla, wout = (a.astype(BF16) for a in (proj_sb, proj_gdn, proj_gla, w_out))

    x2 = x.reshape(t, d)
    for layer in range(n_l):
        proj, small = _in_call(x2, mod, r3(norm_g), w_main, w_small, layer, seq, tm=1024, tn=2048)
        proj3 = proj.reshape(b, seq, N_MAIN)
        small3 = small.reshape(b, seq, LANES)
        ysb = _sb_call(proj3, r3(sb_qnorm), r3(sb_knorm), layer)
        ygdn = _gdn_call(proj3, small3, gdn_conv, hp, r3(gdn_onorm), layer)
        ygla = _gla_call(proj3, small3, w2p, r3(gla_b), r3(gla_onorm), layer)
        x2 = _out_call(ysb.reshape(t, -1), ygdn.reshape(t, -1), ygla.reshape(t, -1), proj,
                       r3(merge_b), psb, pgdn, pgla, wout, x2, mod, layer, seq, tm=512)
    return x2.reshape(b, seq, d)
```

```python
import functools

import jax
import jax.numpy as jnp
from jax import lax
from jax.experimental import pallas as pl
from jax.experimental.pallas import tpu as pltpu

F32 = jnp.float32
BF16 = jnp.bfloat16

D_MODEL = 1024
N_HEADS = 4
HEAD_DIM = 128
GLA_VD = 256
GDN_CONV = 4
GLA_RANK = 16
GLA_TAU = 16.0
N_BRANCH = 3
RMS_EPS = 1e-6
L2_EPS = 1e-6
CHUNK = 64
GDN_UNROLL = 4
GLA_UNROLL = 2
SUB = 16
LANES = 128
LOG2E = 1.4426950408889634
SB_DEAD = -160.0
INV_BASE = 8
VMEM_LIMIT = 56 * 1024 * 1024

OFF_MG = 0
OFF_DQKV = 3072
OFF_DZ = 4608
OFF_SQ, OFF_SK, OFF_SV, OFF_SZ = 5120, 5632, 6144, 6656
OFF_LQ, OFF_LK, OFF_LV, OFF_LZ = 7168, 7680, 8192, 9216
N_MAIN = 10240
SM_LR, SM_DB, SM_DA = 0, 16, 20


def _dot(a, b):
    return jnp.dot(a, b, preferred_element_type=F32)


def _dot_nt(a, b):
    return lax.dot_general(a, b, (((1,), (1,)), ((), ())), preferred_element_type=F32)


def _dot_tn(a, b):
    return lax.dot_general(a, b, (((0,), (0,)), ((), ())), preferred_element_type=F32)


def _split(x):
    hi = x.astype(BF16)
    lo = (x - hi.astype(F32)).astype(BF16)
    return hi, lo


def _sigmoid(x):
    return 1.0 / (1.0 + jnp.exp2(x * -LOG2E))


def _softplus(x):
    return jnp.maximum(x, 0.0) + jnp.log(1.0 + jnp.exp(-jnp.abs(x)))


def _iota(shape, axis):
    return lax.broadcasted_iota(jnp.int32, shape, axis)


def _ada_kernel(c_ref, w_ref, b_ref, o_ref):
    c = c_ref[...]
    ca = c * _sigmoid(c)
    o_ref[...] = jnp.dot(ca, w_ref[...], preferred_element_type=F32,
                         precision=lax.Precision.HIGHEST) + b_ref[...]


def _ada_call(c, ada_w, ada_b):
    n_l, d, d3 = ada_w.shape
    b = c.shape[0]
    tn = 768
    return pl.pallas_call(
        _ada_kernel,
        out_shape=jax.ShapeDtypeStruct((n_l, b, d3), F32),
        grid=(n_l, d3 // tn),
        in_specs=[
            pl.BlockSpec((b, d), lambda l, j: (0, 0)),
            pl.BlockSpec((None, d, tn), lambda l, j: (l, 0, j)),
            pl.BlockSpec((None, 1, tn), lambda l, j: (l, 0, j)),
        ],
        out_specs=pl.BlockSpec((None, b, tn), lambda l, j: (l, 0, j)),
        compiler_params=pltpu.CompilerParams(
            dimension_semantics=("arbitrary", "arbitrary"), vmem_limit_bytes=VMEM_LIMIT),
        name="adaln_mod",
    )(c, ada_w, ada_b.reshape(n_l, 1, d3))


def _in_kernel(x_ref, mod_ref, g_ref, w_ref, ws_ref, out_ref, small_ref, h_ref, *, slab):
    tm, d = x_ref.shape

    @pl.when(pl.program_id(1) == 0)
    def _():
        shift = mod_ref[:, 0:d]
        scale1 = 1.0 + mod_ref[:, d:2 * d]
        gain = g_ref[...]

        def body(s, carry):
            rows = pl.ds(pl.multiple_of(s * slab, slab), slab)
            x = x_ref[rows, :]
            y = x * lax.rsqrt(jnp.mean(x * x, axis=-1, keepdims=True) + RMS_EPS) * gain
            h_ref[rows, :] = (y * scale1 + shift).astype(BF16)
            return carry

        lax.fori_loop(0, tm // slab, body, 0)
        small_ref[...] = _dot(h_ref[...], ws_ref[...])

    out_ref[...] = _dot(h_ref[...], w_ref[...]).astype(BF16)


def _in_call(x2, mod, norm_g, w_main, w_small, layer, seq, *, tm, tn):
    t, d = x2.shape
    n_main = w_main.shape[-1]
    tm = min(tm, seq)
    rows_per_seq = seq // tm
    kern = functools.partial(_in_kernel, slab=min(128, tm))
    return pl.pallas_call(
        kern,
        out_shape=(jax.ShapeDtypeStruct((t, n_main), BF16),
                   jax.ShapeDtypeStruct((t, LANES), F32)),
        grid=(t // tm, n_main // tn),
        in_specs=[
            pl.BlockSpec((tm, d), lambda i, j: (i, 0)),
            pl.BlockSpec((None, None, 1, 3 * d), lambda i, j: (layer, i // rows_per_seq, 0, 0)),
            pl.BlockSpec((None, 1, d), lambda i, j: (layer, 0, 0)),
            pl.BlockSpec((None, d, tn), lambda i, j: (layer, 0, j)),
            pl.BlockSpec((None, d, LANES), lambda i, j: (layer, 0, 0)),
        ],
        out_specs=(pl.BlockSpec((tm, tn), lambda i, j: (i, j)),
                   pl.BlockSpec((tm, LANES), lambda i, j: (i, 0))),
        scratch_shapes=[pltpu.VMEM((tm, d), BF16)],
        compiler_params=pltpu.CompilerParams(
            dimension_semantics=("arbitrary", "arbitrary"), vmem_limit_bytes=VMEM_LIMIT),
        name="in_proj",
    )(x2, mod, norm_g, w_main, w_small)


def _sb_kernel(q_ref, k_ref, v_ref, z_ref, qw_ref, kw_ref, o_ref, kn_ref, qn_ref, acc_ref, run_ref,
               *, tq):
    seq = q_ref.shape[0]
    hd = HEAD_DIM
    scale = hd ** -0.5
    heads = range(N_HEADS)
    hcs = [slice(h * hd, (h + 1) * hd) for h in heads]

    def rms(x, w):
        return x * lax.rsqrt(jnp.mean(x * x, axis=-1, keepdims=True) + RMS_EPS) * w

    def knorm(s, carry):
        rows = pl.ds(pl.multiple_of(s * tq, tq), tq)
        for h in heads:
            kn_ref[rows, hcs[h]] = rms(k_ref[rows, hcs[h]].astype(F32), kw_ref[...]).astype(BF16)
        return carry

    lax.fori_loop(0, seq // tq, knorm, 0)

    later = jnp.where(_iota((tq, tq), 0) > _iota((tq, tq), 1), 1.0, 0.0).astype(BF16)
    sign = jnp.int32(-2 ** 31)

    half = tq // 2
    mask_top = _iota((half, half), 1) < _iota((half, half), 0)
    mask_bot = _iota((half, tq), 1) < half + _iota((half, tq), 0)

    def run_jobs(jobs, first):
        z = [_dot_nt(qn_ref[h, rs], kn_ref[keys, hcs[h]]) for h, rs, keys, nk, m in jobs]
        t = [jnp.log2(1.0 + jnp.exp2(pltpu.bitcast(pltpu.bitcast(x, jnp.int32) | sign, F32))) for x in z]
        lb = [jnp.minimum(x, 0.0) - y for x, y in zip(z, t)]
        lg = [x - y for x, y in zip(lb, z)]
        lg = [x if job[4] is None else jnp.where(job[4], x, 0.0) for x, job in zip(lg, jobs)]
        tail = [_dot(x.astype(BF16), later[:job[3], :job[3]]) for x, job in zip(lg, jobs)]
        ws = []
        for (h, rs, keys, nk, m), lbi, lgi, ti in zip(jobs, lb, lg, tail):
            tot = jnp.sum(lgi, axis=1, keepdims=True)
            if first:
                w = jnp.exp2(lbi + ti)
                run_ref[h, rs] = tot
            else:
                run = run_ref[h, rs]
                w = jnp.exp2(lbi + ti + run)
                run_ref[h, rs] = run + tot
            if m is not None:
                w = jnp.where(m, w, 0.0)
            ws.append(w.astype(BF16))
        for (h, rs, keys, nk, m), w in zip(jobs, ws):
            pv = _dot(w, v_ref[keys, hcs[h]])
            if first:
                acc_ref[h, rs] = pv
            else:
                acc_ref[h, rs] += pv

    def q_block(i, carry):
        r0 = pl.multiple_of(i * tq, tq)
        rows = pl.ds(r0, tq)
        for h in heads:
            qn_ref[h] = (rms(q_ref[rows, hcs[h]].astype(F32), qw_ref[...]) * (scale * LOG2E)).astype(BF16)
        run_jobs([(h, slice(0, half), pl.ds(r0, half), half, mask_top) for h in heads]
                 + [(h, slice(half, tq), pl.ds(r0, tq), tq, mask_bot) for h in heads], True)

        def alive():
            return jnp.max(run_ref[...]) > SB_DEAD

        def below(carry):
            n, _ = carry
            keys = pl.ds(pl.multiple_of((i - 1 - n) * tq, tq), tq)
            run_jobs([(h, slice(0, tq), keys, tq, None) for h in heads], False)
            return n + 1, alive()

        lax.while_loop(lambda carry: (carry[0] < i) & carry[1], below, (0, alive()))
        for h in heads:
            zg = z_ref[rows, hcs[h]].astype(F32)
            o_ref[rows, hcs[h]] = (acc_ref[h] * (zg * _sigmoid(zg))).astype(BF16)
        return carry

    lax.fori_loop(0, seq // tq, q_block, 0)


def _sb_call(proj3, qw, kw, layer, *, tq=256):
    b, seq, _ = proj3.shape
    hd = HEAD_DIM
    w = N_HEADS * hd

    def col(off):
        return pl.BlockSpec((None, seq, w), lambda bi: (bi, 0, off // w))

    wspec = pl.BlockSpec((None, 1, hd), lambda bi: (layer, 0, 0))
    return pl.pallas_call(
        functools.partial(_sb_kernel, tq=tq),
        out_shape=jax.ShapeDtypeStruct((b, seq, w), BF16),
        grid=(b,),
        in_specs=[col(OFF_SQ), col(OFF_SK), col(OFF_SV), col(OFF_SZ), wspec, wspec],
        out_specs=pl.BlockSpec((None, seq, w), lambda bi: (bi, 0, 0)),
        scratch_shapes=[pltpu.VMEM((seq, w), BF16),
                        pltpu.VMEM((N_HEADS, tq, hd), BF16),
                        pltpu.VMEM((N_HEADS, tq, hd), F32),
                        pltpu.VMEM((N_HEADS, tq, 1), F32)],
        compiler_params=pltpu.CompilerParams(
            dimension_semantics=("arbitrary",), vmem_limit_bytes=VMEM_LIMIT),
        name="sb_attn",
    )(proj3, proj3, proj3, proj3, qw, kw)


def _unit_lower_inverse(a_list, ri, ci):
    size = ri.shape[0]
    same = lambda w: (ri // w) == (ci // w)
    eye = jnp.where(ri == ci, 1.0, 0.0).astype(F32)
    d = [jnp.where(same(INV_BASE), a, 0.0) for a in a_list]
    x = [eye - y for y in d]
    p = d
    w = 2
    while w < INV_BASE:
        pb = [y.astype(BF16) for y in p]
        p = [_dot(y, y) for y in pb]
        x = [y + _dot(y.astype(BF16), z.astype(BF16)) for y, z in zip(x, p)]
        w *= 2
    w = INV_BASE
    while w < size:
        below = same(2 * w) & jnp.logical_not(same(w))
        xb = [y.astype(BF16) for y in x]
        lx = [_dot(jnp.where(below, a, 0.0).astype(BF16), y) for a, y in zip(a_list, xb)]
        x = [y - _dot(yb, z.astype(BF16)) for y, yb, z in zip(x, xb, lx)]
        w *= 2
    return x


def _gdn_kernel(qkv_ref, prev_ref, dz_ref, sm_ref, conv_ref, hp_ref, on_ref, o_ref,
                xs_ref, st_ref):
    rblk = qkv_ref.shape[0]
    hd = HEAD_DIM
    c = CHUNK
    halo = prev_ref.shape[0]
    blk = pl.program_id(1)

    @pl.when(blk == 0)
    def _():
        st_ref[...] = jnp.zeros_like(st_ref)
        xs_ref[0:halo, :] = jnp.zeros((halo, xs_ref.shape[1]), F32)

    @pl.when(blk > 0)
    def _():
        xs_ref[0:halo, :] = prev_ref[...].astype(F32)

    def stage(s, carry):
        rows = pl.multiple_of(s * c, c)
        xs_ref[pl.ds(halo + rows, c), :] = qkv_ref[pl.ds(rows, c), :].astype(F32)
        return carry

    lax.fori_loop(0, rblk // c, stage, 0)

    ri = _iota((c, c), 0)
    ci = _iota((c, c), 1)
    tri_incl = ri >= ci
    tri_strict = ri > ci
    tri2 = jnp.where(_iota((c, 2 * c), 0) >= (_iota((c, 2 * c), 1) % c), 1.0, 0.0).astype(BF16)
    later = jnp.where(_iota((c, LANES), 0) > _iota((c, LANES), 1), 1.0, 0.0).astype(F32)

    nu = GDN_UNROLL
    heads = range(N_HEADS)
    hcs = [slice(h * hd, (h + 1) * hd) for h in heads]
    chains = [(u, h) for u in range(nu) for h in heads]

    def conv(win, col0):
        w = xs_ref[win, col0:col0 + hd]
        cw = conv_ref[:, col0:col0 + hd]
        y = w[halo:] * cw[GDN_CONV - 1:GDN_CONV]
        for j in range(1, GDN_CONV):
            y = y + pltpu.roll(w, j, axis=0)[halo:] * cw[GDN_CONV - 1 - j:GDN_CONV - j]
        return y * _sigmoid(y)

    def chunks(n, carry):
        base = n * (nu * c)
        r0 = [pl.multiple_of(base + u * c, c) for u in range(nu)]
        rows = [pl.ds(r0[u], c) for u in range(nu)]
        win = [pl.ds(r0[u], c + halo) for u in range(nu)]
        sm = [sm_ref[rows[u], :] for u in range(nu)]

        q = [conv(win[u], h * hd) for u, h in chains]
        k = [conv(win[u], N_HEADS * hd + h * hd) for u, h in chains]
        v = [conv(win[u], 2 * N_HEADS * hd + h * hd) for u, h in chains]
        q = [x * (lax.rsqrt(jnp.sum(x * x, axis=-1, keepdims=True) + L2_EPS) * (hd ** -0.5))
             for x in q]
        k = [x * lax.rsqrt(jnp.sum(x * x, axis=-1, keepdims=True) + L2_EPS) for x in k]
        beta = [_sigmoid(jnp.broadcast_to(sm[u][:, SM_DB + h:SM_DB + h + 1], (c, hd)))
                for u, h in chains]
        g = [-jnp.exp(hp_ref[0:1, hcs[h]])
             * _softplus(jnp.broadcast_to(sm[u][:, SM_DA + h:SM_DA + h + 1], (c, hd))
                         + hp_ref[1:2, hcs[h]])
             for u, h in chains]

        cum = []
        for x in g:
            xh, xl = _split(jnp.concatenate([x, x * later], axis=1))
            cum.append(_dot(tri2, jnp.concatenate([xh, xl], axis=0)))
        gam = [x[:, :hd] for x in cum]
        decay = [jnp.where(tri_incl, jnp.exp(x[:, hd:hd + c]), 0.0) for x in cum]
        eg = [jnp.exp(x) for x in gam]
        gl = [x[c - 1:c, :] for x in gam]

        kb = [x.astype(BF16) for x in k]
        kk = [_dot_nt(x, x) for x in kb]
        qk = [(_dot_nt(q[i].astype(BF16), kb[i]) * decay[i]).astype(BF16) for i in range(len(chains))]
        a = [jnp.where(tri_strict, beta[i][:, :c] * kk[i] * decay[i], 0.0) for i in range(len(chains))]
        tinv = _unit_lower_inverse(a, ri, ci)
        sol = [_dot(tinv[i].astype(BF16),
                    jnp.concatenate([v[i] * beta[i], k[i] * (beta[i] * eg[i])], axis=1).astype(BF16))
               for i in range(len(chains))]
        lhs = [jnp.concatenate([sol[i][:, hd:], q[i] * eg[i]], axis=0).astype(BF16)
               for i in range(len(chains))]
        kd = [(k[i] * jnp.exp(gl[i] - gam[i])).astype(BF16) for i in range(len(chains))]
        cd = [jnp.exp(x) for x in gl]

        state = [st_ref[h] for h in heads]
        outs = []
        for u in range(nu):
            idx = [u * N_HEADS + h for h in heads]
            r = [_dot(lhs[i], state[h].astype(BF16)) for h, i in zip(heads, idx)]
            ub = [(sol[i][:, :hd] - r[h][:c]).astype(BF16) for h, i in zip(heads, idx)]
            outs.extend(r[h][c:] + _dot(qk[i], ub[h]) for h, i in zip(heads, idx))
            state = [state[h] * cd[i] + _dot_tn(kd[i], ub[h]) for h, i in zip(heads, idx)]
        for h in heads:
            st_ref[h] = state[h]
        for i, (u, h) in enumerate(chains):
            o = outs[i]
            on = o * lax.rsqrt(jnp.mean(o * o, axis=-1, keepdims=True) + RMS_EPS) * on_ref[...]
            zg = dz_ref[rows[u], hcs[h]].astype(F32)
            o_ref[rows[u], hcs[h]] = (on * (zg * _sigmoid(zg))).astype(BF16)
        return carry

    lax.fori_loop(0, rblk // (nu * c), chunks, 0)


def _gdn_call(proj3, small3, conv_w, hp, onorm, layer, *, rblk=1024, halo=16):
    b, seq, _ = proj3.shape
    rblk = min(rblk, seq)
    w3 = 3 * N_HEADS * HEAD_DIM
    wz = N_HEADS * HEAD_DIM
    per = rblk // halo
    return pl.pallas_call(
        _gdn_kernel,
        out_shape=jax.ShapeDtypeStruct((b, seq, wz), BF16),
        grid=(b, seq // rblk),
        in_specs=[
            pl.BlockSpec((None, rblk, w3), lambda bi, i: (bi, i, OFF_DQKV // w3)),
            pl.BlockSpec((None, halo, w3),
                         lambda bi, i: (bi, jnp.maximum(i * per - 1, 0), OFF_DQKV // w3)),
            pl.BlockSpec((None, rblk, wz), lambda bi, i: (bi, i, OFF_DZ // wz)),
            pl.BlockSpec((None, rblk, LANES), lambda bi, i: (bi, i, 0)),
            pl.BlockSpec((None, GDN_CONV, w3), lambda bi, i: (layer, 0, 0)),
            pl.BlockSpec((None, 2, wz), lambda bi, i: (layer, 0, 0)),
            pl.BlockSpec((None, 1, HEAD_DIM), lambda bi, i: (layer, 0, 0)),
        ],
        out_specs=pl.BlockSpec((None, rblk, wz), lambda bi, i: (bi, i, 0)),
        scratch_shapes=[pltpu.VMEM((rblk + halo, w3), F32),
                        pltpu.VMEM((N_HEADS, HEAD_DIM, HEAD_DIM), F32)],
        compiler_params=pltpu.CompilerParams(
            dimension_semantics=("arbitrary", "arbitrary"), vmem_limit_bytes=VMEM_LIMIT),
        name="gdn",
    )(proj3, proj3, proj3, small3, conv_w, hp, onorm)


def _gla_kernel(q_ref, k_ref, v_ref, z_ref, sm_ref, w2_ref, b_ref, on_ref, o_ref, st_ref):
    rblk = q_ref.shape[0]
    hd = HEAD_DIM
    vd = GLA_VD
    c = CHUNK

    @pl.when(pl.program_id(1) == 0)
    def _():
        st_ref[...] = jnp.zeros_like(st_ref)

    tri_incl = _iota((c, c), 0) >= _iota((c, c), 1)
    tri2 = jnp.where(_iota((c, 2 * c), 0) >= (_iota((c, 2 * c), 1) % c), 1.0, 0.0).astype(BF16)
    nu = GLA_UNROLL
    heads = range(N_HEADS)
    hcs = [slice(h * hd, (h + 1) * hd) for h in heads]
    vcs = [slice(h * vd, (h + 1) * vd) for h in heads]
    chains = [(u, h) for u in range(nu) for h in heads]
    nc = len(chains)

    def chunks(n, carry):
        base = n * (nu * c)
        rows = [pl.ds(pl.multiple_of(base + u * c, c), c) for u in range(nu)]
        la = []
        for u in range(nu):
            x = _dot(sm_ref[rows[u], :].astype(BF16), w2_ref[...]) + b_ref[...]
            la.append(-_softplus(-x) * (1.0 / GLA_TAU))
        gc = []
        for u, h in chains:
            lh, ll = _split(la[u][:, hcs[h]])
            gc.append(_dot(tri2, jnp.concatenate([lh, ll], axis=0)))
        gl = [x[c - 1:c, :] for x in gc]
        q = [q_ref[rows[u], hcs[h]].astype(F32) * (hd ** -0.5) for u, h in chains]
        k = [k_ref[rows[u], hcs[h]].astype(F32) for u, h in chains]
        v = [v_ref[rows[u], vcs[h]] for u, h in chains]
        qd = [(q[i] * jnp.exp(gc[i])).astype(BF16) for i in range(nc)]
        att = []
        for i in range(nc):
            parts = []
            for j in range(c // SUB):
                rs = slice(j * SUB, (j + 1) * SUB)
                ref_g = gc[i][j * SUB:j * SUB + 1, :]
                qi = (q[i][rs] * jnp.exp(gc[i][rs] - ref_g)).astype(BF16)
                ki = (k[i] * jnp.exp(jnp.minimum(ref_g - gc[i], 60.0))).astype(BF16)
                parts.append(_dot_nt(qi, ki))
            att.append(jnp.where(tri_incl, jnp.concatenate(parts, axis=0), 0.0).astype(BF16))
        intra = [_dot(att[i], v[i]) for i in range(nc)]
        grow = [_dot_tn(v[i], (k[i] * jnp.exp(gl[i] - gc[i])).astype(BF16)) for i in range(nc)]
        keep = [jnp.exp(x) for x in gl]

        state = [st_ref[h] for h in heads]
        outs = []
        for u in range(nu):
            idx = [u * N_HEADS + h for h in heads]
            outs.extend(intra[i] + _dot_nt(qd[i], state[h].astype(BF16)) for h, i in zip(heads, idx))
            state = [state[h] * keep[i] + grow[i] for h, i in zip(heads, idx)]
        for h in heads:
            st_ref[h] = state[h]
        for i, (u, h) in enumerate(chains):
            o = outs[i]
            on = o * lax.rsqrt(jnp.mean(o * o, axis=-1, keepdims=True) + RMS_EPS) * on_ref[...]
            zg = z_ref[rows[u], vcs[h]].astype(F32)
            o_ref[rows[u], vcs[h]] = (on * (zg * _sigmoid(zg))).astype(BF16)
        return carry

    lax.fori_loop(0, rblk // (nu * c), chunks, 0)


def _gla_call(proj3, small3, w2p, gla_b, onorm, layer, *, rblk=1024):
    b, seq, _ = proj3.shape
    rblk = min(rblk, seq)
    wk = N_HEADS * HEAD_DIM
    wv = N_HEADS * GLA_VD
    return pl.pallas_call(
        _gla_kernel,
        out_shape=jax.ShapeDtypeStruct((b, seq, wv), BF16),
        grid=(b, seq // rblk),
        in_specs=[
            pl.BlockSpec((None, rblk, wk), lambda bi, i: (bi, i, OFF_LQ // wk)),
            pl.BlockSpec((None, rblk, wk), lambda bi, i: (bi, i, OFF_LK // wk)),
            pl.BlockSpec((None, rblk, wv), lambda bi, i: (bi, i, OFF_LV // wv)),
            pl.BlockSpec((None, rblk, wv), lambda bi, i: (bi, i, OFF_LZ // wv)),
            pl.BlockSpec((None, rblk, LANES), lambda bi, i: (bi, i, 0)),
            pl.BlockSpec((None, LANES, wk), lambda bi, i: (layer, 0, 0)),
            pl.BlockSpec((None, 1, wk), lambda bi, i: (layer, 0, 0)),
            pl.BlockSpec((None, 1, GLA_VD), lambda bi, i: (layer, 0, 0)),
        ],
        out_specs=pl.BlockSpec((None, rblk, wv), lambda bi, i: (bi, i, 0)),
        scratch_shapes=[pltpu.VMEM((N_HEADS, GLA_VD, HEAD_DIM), F32)],
        compiler_params=pltpu.CompilerParams(
            dimension_semantics=("arbitrary", "arbitrary"), vmem_limit_bytes=VMEM_LIMIT),
        name="gla",
    )(proj3, proj3, proj3, proj3, small3, w2p, gla_b, onorm)


def _out_kernel(ysb_ref, ygdn_ref, ygla_ref, mg_ref, mb_ref, psb_ref, pgdn_ref, pgla_ref,
                wout_ref, x_ref, mod_ref, o_ref):
    d = x_ref.shape[1]
    gates = _sigmoid(mg_ref[...].astype(F32) + mb_ref[...])
    merged = gates[:, 0:d] * _dot(ysb_ref[...], psb_ref[...])
    merged = merged + gates[:, d:2 * d] * _dot(ygdn_ref[...], pgdn_ref[...])
    merged = merged + gates[:, 2 * d:3 * d] * _dot(ygla_ref[...], pgla_ref[...])
    o_ref[...] = x_ref[...] + mod_ref[:, 2 * d:3 * d] * _dot(merged.astype(BF16), wout_ref[...])


def _out_call(ysb, ygdn, ygla, proj, merge_b, psb, pgdn, pgla, wout, x2, mod, layer, seq, *, tm):
    t, d = x2.shape
    tm = min(tm, seq)
    rows_per_seq = seq // tm
    wsb, wgdn, wgla = ysb.shape[1], ygdn.shape[1], ygla.shape[1]

    def wspec(k):
        return pl.BlockSpec((None, k, d), lambda i: (layer, 0, 0))

    return pl.pallas_call(
        _out_kernel,
        out_shape=jax.ShapeDtypeStruct((t, d), F32),
        grid=(t // tm,),
        in_specs=[
            pl.BlockSpec((tm, wsb), lambda i: (i, 0)),
            pl.BlockSpec((tm, wgdn), lambda i: (i, 0)),
            pl.BlockSpec((tm, wgla), lambda i: (i, 0)),
            pl.BlockSpec((tm, N_BRANCH * d), lambda i: (i, OFF_MG // (N_BRANCH * d))),
            pl.BlockSpec((None, 1, N_BRANCH * d), lambda i: (layer, 0, 0)),
            wspec(wsb), wspec(wgdn), wspec(wgla), wspec(d),
            pl.BlockSpec((tm, d), lambda i: (i, 0)),
            pl.BlockSpec((None, None, 1, 3 * d), lambda i: (layer, i // rows_per_seq, 0, 0)),
        ],
        out_specs=pl.BlockSpec((tm, d), lambda i: (i, 0)),
        compiler_params=pltpu.CompilerParams(
            dimension_semantics=("arbitrary",), vmem_limit_bytes=VMEM_LIMIT),
        name="merge_out",
    )(ysb, ygdn, ygla, proj, merge_b, psb, pgdn, pgla, wout, x2, mod)


def _reorder_w_in(w_in):
    hw = N_HEADS * HEAD_DIM
    sb = w_in[..., 0:4 * hw]
    dqkv = w_in[..., 4 * hw:7 * hw]
    dz = w_in[..., 7 * hw:8 * hw]
    o = 8 * hw
    db = w_in[..., o:o + N_HEADS]
    da = w_in[..., o + N_HEADS:o + 2 * N_HEADS]
    o += 2 * N_HEADS
    lqk = w_in[..., o:o + 2 * hw]
    lvz = w_in[..., o + 2 * hw:o + 2 * hw + 2 * N_HEADS * GLA_VD]
    o += 2 * hw + 2 * N_HEADS * GLA_VD
    lr = w_in[..., o:o + GLA_RANK]
    mg = w_in[..., o + GLA_RANK:]
    main = jnp.concatenate([mg, dqkv, dz, sb, lqk, lvz], axis=-1).astype(BF16)
    pad = jnp.zeros(w_in.shape[:-1] + (LANES - GLA_RANK - 2 * N_HEADS,), w_in.dtype)
    small = jnp.concatenate([lr, db, da, pad], axis=-1).astype(BF16)
    return main, small


def kernel(x, c, ada_w, ada_b, norm_g, w_in, sb_qnorm, sb_knorm, gdn_conv, gdn_a_log, gdn_dt_bias,
           gdn_onorm, gla_w2, gla_b, gla_onorm, merge_b, proj_sb, proj_gdn, proj_gla, w_out):
    b, seq, d = x.shape
    n_l = w_in.shape[0]
    t = b * seq

    w_main, w_small = _reorder_w_in(w_in)
    assert w_main.shape[-1] == N_MAIN
    mod = _ada_call(c, ada_w, ada_b).reshape(n_l, b, 1, 3 * d)
    hp = jnp.stack([jnp.repeat(gdn_a_log, HEAD_DIM, axis=-1),
                    jnp.repeat(gdn_dt_bias, HEAD_DIM, axis=-1)], axis=1)
    w2p = jnp.pad(gla_w2, ((0, 0), (SM_LR, LANES - SM_LR - GLA_RANK), (0, 0))).astype(BF16)
    r3 = lambda a: a.reshape(n_l, 1, a.shape[-1])
    psb, pgdn, pgla, wout = (a.astype(BF16) for a in (proj_sb, proj_gdn, proj_gla, w_out))

    x2 = x.reshape(t, d)
    for layer in range(n_l):
        proj, small = _in_call(x2, mod, r3(norm_g), w_main, w_small, layer, seq, tm=1024, tn=2560)
        proj3 = proj.reshape(b, seq, N_MAIN)
        small3 = small.reshape(b, seq, LANES)
        ysb = _sb_call(proj3, r3(sb_qnorm), r3(sb_knorm), layer)
        ygdn = _gdn_call(proj3, small3, gdn_conv, hp, r3(gdn_onorm), layer)
        ygla = _gla_call(proj3, small3, w2p, r3(gla_b), r3(gla_onorm), layer)
        x2 = _out_call(ysb.reshape(t, -1), ygdn.reshape(t, -1), ygla.reshape(t, -1), proj,
                       r3(merge_b), psb, pgdn, pgla, wout, x2, mod, layer, seq, tm=512)
    return x2.reshape(b, seq, d)
```

```python
import functools

import jax
import jax.numpy as jnp
from jax import lax
from jax.experimental import pallas as pl
from jax.experimental.pallas import tpu as pltpu

F32 = jnp.float32
BF16 = jnp.bfloat16

D_MODEL = 1024
N_HEADS = 4
HEAD_DIM = 128
GLA_VD = 256
GDN_CONV = 4
GLA_RANK = 16
GLA_TAU = 16.0
N_BRANCH = 3
RMS_EPS = 1e-6
L2_EPS = 1e-6
CHUNK = 64
GDN_UNROLL = 8
GLA_UNROLL = 4
SUB = 16
LANES = 128
LOG2E = 1.4426950408889634
SB_DEAD = -160.0
INV_BASE = 8
VMEM_LIMIT = 56 * 1024 * 1024

OFF_MG = 0
OFF_DQKV = 3072
OFF_DZ = 4608
OFF_SQ, OFF_SK, OFF_SV, OFF_SZ = 5120, 5632, 6144, 6656
OFF_LQ, OFF_LK, OFF_LV, OFF_LZ = 7168, 7680, 8192, 9216
N_MAIN = 10240
SM_LR, SM_DB, SM_DA = 0, 16, 20


def _dot(a, b):
    return jnp.dot(a, b, preferred_element_type=F32)


def _dot_nt(a, b):
    return lax.dot_general(a, b, (((1,), (1,)), ((), ())), preferred_element_type=F32)


def _dot_tn(a, b):
    return lax.dot_general(a, b, (((0,), (0,)), ((), ())), preferred_element_type=F32)


def _split(x):
    hi = x.astype(BF16)
    lo = (x - hi.astype(F32)).astype(BF16)
    return hi, lo


def _sigmoid(x):
    return 1.0 / (1.0 + jnp.exp2(x * -LOG2E))


def _softplus(x):
    return jnp.maximum(x, 0.0) + jnp.log(1.0 + jnp.exp(-jnp.abs(x)))


def _iota(shape, axis):
    return lax.broadcasted_iota(jnp.int32, shape, axis)


def _ada_kernel(c_ref, w_ref, b_ref, o_ref):
    c = c_ref[...]
    ca = c * _sigmoid(c)
    o_ref[...] = jnp.dot(ca, w_ref[...], preferred_element_type=F32,
                         precision=lax.Precision.HIGHEST) + b_ref[...]


def _ada_call(c, ada_w, ada_b):
    n_l, d, d3 = ada_w.shape
    b = c.shape[0]
    tn = 768
    return pl.pallas_call(
        _ada_kernel,
        out_shape=jax.ShapeDtypeStruct((n_l, b, d3), F32),
        grid=(n_l, d3 // tn),
        in_specs=[
            pl.BlockSpec((b, d), lambda l, j: (0, 0)),
            pl.BlockSpec((None, d, tn), lambda l, j: (l, 0, j)),
            pl.BlockSpec((None, 1, tn), lambda l, j: (l, 0, j)),
        ],
        out_specs=pl.BlockSpec((None, b, tn), lambda l, j: (l, 0, j)),
        compiler_params=pltpu.CompilerParams(
            dimension_semantics=("arbitrary", "arbitrary"), vmem_limit_bytes=VMEM_LIMIT),
        name="adaln_mod",
    )(c, ada_w, ada_b.reshape(n_l, 1, d3))


def _in_kernel(x_ref, mod_ref, g_ref, w_ref, ws_ref, out_ref, small_ref, h_ref, *, slab):
    tm, d = x_ref.shape

    @pl.when(pl.program_id(1) == 0)
    def _():
        shift = mod_ref[:, 0:d]
        scale1 = 1.0 + mod_ref[:, d:2 * d]
        gain = g_ref[...]

        def body(s, carry):
            rows = pl.ds(pl.multiple_of(s * slab, slab), slab)
            x = x_ref[rows, :]
            y = x * lax.rsqrt(jnp.mean(x * x, axis=-1, keepdims=True) + RMS_EPS) * gain
            h_ref[rows, :] = (y * scale1 + shift).astype(BF16)
            return carry

        lax.fori_loop(0, tm // slab, body, 0)
        small_ref[...] = _dot(h_ref[...], ws_ref[...])

    out_ref[...] = _dot(h_ref[...], w_ref[...]).astype(BF16)


def _in_call(x2, mod, norm_g, w_main, w_small, layer, seq, *, tm, tn):
    t, d = x2.shape
    n_main = w_main.shape[-1]
    tm = min(tm, seq)
    rows_per_seq = seq // tm
    kern = functools.partial(_in_kernel, slab=min(128, tm))
    return pl.pallas_call(
        kern,
        out_shape=(jax.ShapeDtypeStruct((t, n_main), BF16),
                   jax.ShapeDtypeStruct((t, LANES), F32)),
        grid=(t // tm, n_main // tn),
        in_specs=[
            pl.BlockSpec((tm, d), lambda i, j: (i, 0)),
            pl.BlockSpec((None, None, 1, 3 * d), lambda i, j: (layer, i // rows_per_seq, 0, 0)),
            pl.BlockSpec((None, 1, d), lambda i, j: (layer, 0, 0)),
            pl.BlockSpec((None, d, tn), lambda i, j: (layer, 0, j)),
            pl.BlockSpec((None, d, LANES), lambda i, j: (layer, 0, 0)),
        ],
        out_specs=(pl.BlockSpec((tm, tn), lambda i, j: (i, j)),
                   pl.BlockSpec((tm, LANES), lambda i, j: (i, 0))),
        scratch_shapes=[pltpu.VMEM((tm, d), BF16)],
        compiler_params=pltpu.CompilerParams(
            dimension_semantics=("arbitrary", "arbitrary"), vmem_limit_bytes=VMEM_LIMIT),
        name="in_proj",
    )(x2, mod, norm_g, w_main, w_small)


def _sb_kernel(q_ref, k_ref, v_ref, z_ref, qw_ref, kw_ref, o_ref, kn_ref, qn_ref, acc_ref, run_ref,
               *, tq):
    nb, seq = q_ref.shape[0], q_ref.shape[1]
    hd = HEAD_DIM
    scale = hd ** -0.5
    heads = range(N_HEADS)
    hcs = [slice(h * hd, (h + 1) * hd) for h in heads]
    chains = [(b, h, b * N_HEADS + h) for b in range(nb) for h in heads]

    def rms(x, w):
        return x * lax.rsqrt(jnp.mean(x * x, axis=-1, keepdims=True) + RMS_EPS) * w

    def knorm(s, carry):
        rows = pl.ds(pl.multiple_of(s * tq, tq), tq)
        for b, h, _ in chains:
            kn_ref[b, rows, hcs[h]] = rms(k_ref[b, rows, hcs[h]].astype(F32), kw_ref[...]).astype(BF16)
        return carry

    lax.fori_loop(0, seq // tq, knorm, 0)

    later = jnp.where(_iota((tq, tq), 0) > _iota((tq, tq), 1), 1.0, 0.0).astype(BF16)
    half = tq // 2
    mask_top = _iota((half, half), 1) < _iota((half, half), 0)
    mask_bot = _iota((half, tq), 1) < half + _iota((half, tq), 0)

    def run_jobs(jobs, first):
        z = [_dot_nt(qn_ref[s, rs], kn_ref[b, keys, hcs[h]]) for (b, h, s), rs, keys, nk, m in jobs]
        t = [jnp.log2(1.0 + jnp.exp2(-jnp.abs(x))) for x in z]
        lb = [jnp.minimum(x, 0.0) - y for x, y in zip(z, t)]
        lg = [x - y for x, y in zip(lb, z)]
        lg = [x if job[4] is None else jnp.where(job[4], x, 0.0) for x, job in zip(lg, jobs)]
        tail = [_dot(x.astype(BF16), later[:job[3], :job[3]]) for x, job in zip(lg, jobs)]
        ws = []
        for ((b, h, s), rs, keys, nk, m), lbi, lgi, ti in zip(jobs, lb, lg, tail):
            tot = jnp.sum(lgi, axis=1, keepdims=True)
            if first:
                w = jnp.exp2(lbi + ti)
                run_ref[s, rs] = tot
            else:
                run = run_ref[s, rs]
                w = jnp.exp2(lbi + ti + run)
                run_ref[s, rs] = run + tot
            if m is not None:
                w = jnp.where(m, w, 0.0)
            ws.append(w.astype(BF16))
        for ((b, h, s), rs, keys, nk, m), w in zip(jobs, ws):
            pv = _dot(w, v_ref[b, keys, hcs[h]])
            if first:
                acc_ref[s, rs] = pv
            else:
                acc_ref[s, rs] += pv

    def q_block(i, carry):
        r0 = pl.multiple_of(i * tq, tq)
        rows = pl.ds(r0, tq)
        for b, h, s in chains:
            qn_ref[s] = (rms(q_ref[b, rows, hcs[h]].astype(F32), qw_ref[...]) * (scale * LOG2E)).astype(BF16)
        run_jobs([(ch, slice(0, half), pl.ds(r0, half), half, mask_top) for ch in chains]
                 + [(ch, slice(half, tq), pl.ds(r0, tq), tq, mask_bot) for ch in chains], True)

        def alive():
            return jnp.max(run_ref[...]) > SB_DEAD

        def below(carry):
            n, _ = carry
            keys = pl.ds(pl.multiple_of((i - 1 - n) * tq, tq), tq)
            run_jobs([(ch, slice(0, tq), keys, tq, None) for ch in chains], False)
            return n + 1, alive()

        lax.while_loop(lambda carry: (carry[0] < i) & carry[1], below, (0, alive()))
        for b, h, s in chains:
            zg = z_ref[b, rows, hcs[h]].astype(F32)
            o_ref[b, rows, hcs[h]] = (acc_ref[s] * (zg * _sigmoid(zg))).astype(BF16)
        return carry

    lax.fori_loop(0, seq // tq, q_block, 0)


def _sb_call(proj3, qw, kw, layer, *, tq=256, nb=2):
    b, seq, _ = proj3.shape
    hd = HEAD_DIM
    w = N_HEADS * hd
    nb = min(nb, b)

    def col(off):
        return pl.BlockSpec((nb, seq, w), lambda bi: (bi, 0, off // w))

    wspec = pl.BlockSpec((None, 1, hd), lambda bi: (layer, 0, 0))
    return pl.pallas_call(
        functools.partial(_sb_kernel, tq=tq),
        out_shape=jax.ShapeDtypeStruct((b, seq, w), BF16),
        grid=(b // nb,),
        in_specs=[col(OFF_SQ), col(OFF_SK), col(OFF_SV), col(OFF_SZ), wspec, wspec],
        out_specs=pl.BlockSpec((nb, seq, w), lambda bi: (bi, 0, 0)),
        scratch_shapes=[pltpu.VMEM((nb, seq, w), BF16),
                        pltpu.VMEM((nb * N_HEADS, tq, hd), BF16),
                        pltpu.VMEM((nb * N_HEADS, tq, hd), F32),
                        pltpu.VMEM((nb * N_HEADS, tq, 1), F32)],
        compiler_params=pltpu.CompilerParams(
            dimension_semantics=("arbitrary",), vmem_limit_bytes=VMEM_LIMIT),
        name="sb_attn",
    )(proj3, proj3, proj3, proj3, qw, kw)


def _unit_lower_inverse(a_list, ri, ci):
    size = ri.shape[0]
    same = lambda w: (ri // w) == (ci // w)
    eye = jnp.where(ri == ci, 1.0, 0.0).astype(F32)
    d = [jnp.where(same(INV_BASE), a, 0.0) for a in a_list]
    x = [eye - y for y in d]
    p = d
    w = 2
    while w < INV_BASE:
        pb = [y.astype(BF16) for y in p]
        p = [_dot(y, y) for y in pb]
        x = [y + _dot(y.astype(BF16), z.astype(BF16)) for y, z in zip(x, p)]
        w *= 2
    w = INV_BASE
    while w < size:
        below = same(2 * w) & jnp.logical_not(same(w))
        xb = [y.astype(BF16) for y in x]
        lx = [_dot(jnp.where(below, a, 0.0).astype(BF16), y) for a, y in zip(a_list, xb)]
        x = [y - _dot(yb, z.astype(BF16)) for y, yb, z in zip(x, xb, lx)]
        w *= 2
    return x


def _gdn_kernel(qkv_ref, prev_ref, dz_ref, sm_ref, conv_ref, hp_ref, on_ref, o_ref,
                xs_ref, st_ref):
    rblk = qkv_ref.shape[0]
    hd = HEAD_DIM
    c = CHUNK
    halo = prev_ref.shape[0]
    blk = pl.program_id(1)

    @pl.when(blk == 0)
    def _():
        st_ref[...] = jnp.zeros_like(st_ref)
        xs_ref[0:halo, :] = jnp.zeros((halo, xs_ref.shape[1]), F32)

    @pl.when(blk > 0)
    def _():
        xs_ref[0:halo, :] = prev_ref[...].astype(F32)

    def stage(s, carry):
        rows = pl.multiple_of(s * c, c)
        xs_ref[pl.ds(halo + rows, c), :] = qkv_ref[pl.ds(rows, c), :].astype(F32)
        return carry

    lax.fori_loop(0, rblk // c, stage, 0)

    ri = _iota((c, c), 0)
    ci = _iota((c, c), 1)
    tri_incl = ri >= ci
    tri_strict = ri > ci
    tri2 = jnp.where(_iota((c, 2 * c), 0) >= (_iota((c, 2 * c), 1) % c), 1.0, 0.0).astype(BF16)
    later = jnp.where(_iota((c, LANES), 0) > _iota((c, LANES), 1), 1.0, 0.0).astype(F32)

    nu = GDN_UNROLL
    heads = range(N_HEADS)
    hcs = [slice(h * hd, (h + 1) * hd) for h in heads]
    chains = [(u, h) for u in range(nu) for h in heads]

    pre = 8

    def conv(win, col0):
        w = xs_ref[win, col0:col0 + hd]
        cw = conv_ref[:, col0:col0 + hd]
        y = w[pre:] * cw[GDN_CONV - 1:GDN_CONV]
        for j in range(1, GDN_CONV):
            y = y + pltpu.roll(w, j, axis=0)[pre:] * cw[GDN_CONV - 1 - j:GDN_CONV - j]
        return y * _sigmoid(y)

    def chunks(n, carry):
        base = n * (nu * c)
        r0 = [pl.multiple_of(base + u * c, c) for u in range(nu)]
        rows = [pl.ds(r0[u], c) for u in range(nu)]
        win = [pl.ds(pl.multiple_of(r0[u] + (halo - pre), 8), c + pre) for u in range(nu)]
        sm = [sm_ref[rows[u], :] for u in range(nu)]

        q = [conv(win[u], h * hd) for u, h in chains]
        k = [conv(win[u], N_HEADS * hd + h * hd) for u, h in chains]
        v = [conv(win[u], 2 * N_HEADS * hd + h * hd) for u, h in chains]
        q = [x * (lax.rsqrt(jnp.sum(x * x, axis=-1, keepdims=True) + L2_EPS) * (hd ** -0.5))
             for x in q]
        k = [x * lax.rsqrt(jnp.sum(x * x, axis=-1, keepdims=True) + L2_EPS) for x in k]
        bt = [_sigmoid(x) for x in sm]
        gt = [-jnp.exp(hp_ref[0:1, :]) * _softplus(x + hp_ref[1:2, :]) for x in sm]
        beta = [jnp.broadcast_to(bt[u][:, SM_DB + h:SM_DB + h + 1], (c, hd)) for u, h in chains]
        g = [jnp.broadcast_to(gt[u][:, SM_DA + h:SM_DA + h + 1], (c, hd))
             for u, h in chains]

        cum = []
        for x in g:
            xh, xl = _split(jnp.concatenate([x, x * later], axis=1))
            cum.append(_dot(tri2, jnp.concatenate([xh, xl], axis=0)))
        gam = [x[:, :hd] for x in cum]
        decay = [jnp.where(tri_incl, jnp.exp(x[:, hd:hd + c]), 0.0) for x in cum]
        eg = [jnp.exp(x) for x in gam]
        gl = [x[c - 1:c, :] for x in gam]

        kb = [x.astype(BF16) for x in k]
        kk = [_dot_nt(x, x) for x in kb]
        qk = [(_dot_nt(q[i].astype(BF16), kb[i]) * decay[i]).astype(BF16) for i in range(len(chains))]
        a = [jnp.where(tri_strict, beta[i][:, :c] * kk[i] * decay[i], 0.0) for i in range(len(chains))]
        tinv = _unit_lower_inverse(a, ri, ci)
        sol = [_dot(tinv[i].astype(BF16),
                    jnp.concatenate([v[i] * beta[i], k[i] * (beta[i] * eg[i])], axis=1).astype(BF16))
               for i in range(len(chains))]
        lhs = [jnp.concatenate([sol[i][:, hd:], q[i] * eg[i]], axis=0).astype(BF16)
               for i in range(len(chains))]
        kd = [(k[i] * jnp.exp(gl[i] - gam[i])).astype(BF16) for i in range(len(chains))]
        cd = [jnp.exp(x) for x in gl]

        state = [st_ref[h] for h in heads]
        outs = []
        for u in range(nu):
            idx = [u * N_HEADS + h for h in heads]
            r = [_dot(lhs[i], state[h].astype(BF16)) for h, i in zip(heads, idx)]
            ub = [(sol[i][:, :hd] - r[h][:c]).astype(BF16) for h, i in zip(heads, idx)]
            outs.extend(r[h][c:] + _dot(qk[i], ub[h]) for h, i in zip(heads, idx))
            state = [state[h] * cd[i] + _dot_tn(kd[i], ub[h]) for h, i in zip(heads, idx)]
        for h in heads:
            st_ref[h] = state[h]
        for i, (u, h) in enumerate(chains):
            o = outs[i]
            on = o * lax.rsqrt(jnp.mean(o * o, axis=-1, keepdims=True) + RMS_EPS) * on_ref[...]
            zg = dz_ref[rows[u], hcs[h]].astype(F32)
            o_ref[rows[u], hcs[h]] = (on * (zg * _sigmoid(zg))).astype(BF16)
        return carry

    lax.fori_loop(0, rblk // (nu * c), chunks, 0)


def _gdn_call(proj3, small3, conv_w, hp, onorm, layer, *, rblk=1024, halo=16):
    b, seq, _ = proj3.shape
    rblk = min(rblk, seq)
    w3 = 3 * N_HEADS * HEAD_DIM
    wz = N_HEADS * HEAD_DIM
    per = rblk // halo
    return pl.pallas_call(
        _gdn_kernel,
        out_shape=jax.ShapeDtypeStruct((b, seq, wz), BF16),
        grid=(b, seq // rblk),
        in_specs=[
            pl.BlockSpec((None, rblk, w3), lambda bi, i: (bi, i, OFF_DQKV // w3)),
            pl.BlockSpec((None, halo, w3),
                         lambda bi, i: (bi, jnp.maximum(i * per - 1, 0), OFF_DQKV // w3)),
            pl.BlockSpec((None, rblk, wz), lambda bi, i: (bi, i, OFF_DZ // wz)),
            pl.BlockSpec((None, rblk, LANES), lambda bi, i: (bi, i, 0)),
            pl.BlockSpec((None, GDN_CONV, w3), lambda bi, i: (layer, 0, 0)),
            pl.BlockSpec((None, 2, LANES), lambda bi, i: (layer, 0, 0)),
            pl.BlockSpec((None, 1, HEAD_DIM), lambda bi, i: (layer, 0, 0)),
        ],
        out_specs=pl.BlockSpec((None, rblk, wz), lambda bi, i: (bi, i, 0)),
        scratch_shapes=[pltpu.VMEM((rblk + halo, w3), F32),
                        pltpu.VMEM((N_HEADS, HEAD_DIM, HEAD_DIM), F32)],
        compiler_params=pltpu.CompilerParams(
            dimension_semantics=("arbitrary", "arbitrary"), vmem_limit_bytes=VMEM_LIMIT),
        name="gdn",
    )(proj3, proj3, proj3, small3, conv_w, hp, onorm)


def _gla_kernel(q_ref, k_ref, v_ref, z_ref, sm_ref, w2_ref, b_ref, on_ref, o_ref, st_ref):
    rblk = q_ref.shape[0]
    hd = HEAD_DIM
    vd = GLA_VD
    c = CHUNK

    @pl.when(pl.program_id(1) == 0)
    def _():
        st_ref[...] = jnp.zeros_like(st_ref)

    tri_incl = _iota((c, c), 0) >= _iota((c, c), 1)
    tri2 = jnp.where(_iota((c, 2 * c), 0) >= (_iota((c, 2 * c), 1) % c), 1.0, 0.0).astype(BF16)
    nu = GLA_UNROLL
    heads = range(N_HEADS)
    hcs = [slice(h * hd, (h + 1) * hd) for h in heads]
    vcs = [slice(h * vd, (h + 1) * vd) for h in heads]
    chains = [(u, h) for u in range(nu) for h in heads]
    nc = len(chains)

    def chunks(n, carry):
        base = n * (nu * c)
        rows = [pl.ds(pl.multiple_of(base + u * c, c), c) for u in range(nu)]
        la = []
        for u in range(nu):
            x = _dot(sm_ref[rows[u], :].astype(BF16), w2_ref[...]) + b_ref[...]
            la.append(-_softplus(-x) * (1.0 / GLA_TAU))
        gc = []
        for u, h in chains:
            lh, ll = _split(la[u][:, hcs[h]])
            gc.append(_dot(tri2, jnp.concatenate([lh, ll], axis=0)))
        gl = [x[c - 1:c, :] for x in gc]
        q = [q_ref[rows[u], hcs[h]].astype(F32) * (hd ** -0.5) for u, h in chains]
        k = [k_ref[rows[u], hcs[h]].astype(F32) for u, h in chains]
        v = [v_ref[rows[u], vcs[h]] for u, h in chains]
        qd = [(q[i] * jnp.exp(gc[i])).astype(BF16) for i in range(nc)]
        att = []
        for i in range(nc):
            parts = []
            for j in range(c // SUB):
                rs = slice(j * SUB, (j + 1) * SUB)
                ref_g = gc[i][j * SUB:j * SUB + 1, :]
                qi = (q[i][rs] * jnp.exp(gc[i][rs] - ref_g)).astype(BF16)
                ki = (k[i] * jnp.exp(jnp.minimum(ref_g - gc[i], 60.0))).astype(BF16)
                parts.append(_dot_nt(qi, ki))
            att.append(jnp.where(tri_incl, jnp.concatenate(parts, axis=0), 0.0).astype(BF16))
        intra = [_dot(att[i], v[i]) for i in range(nc)]
        grow = [_dot_tn(v[i], (k[i] * jnp.exp(gl[i] - gc[i])).astype(BF16)) for i in range(nc)]
        keep = [jnp.exp(x) for x in gl]

        state = [st_ref[h] for h in heads]
        outs = []
        for u in range(nu):
            idx = [u * N_HEADS + h for h in heads]
            outs.extend(intra[i] + _dot_nt(qd[i], state[h].astype(BF16)) for h, i in zip(heads, idx))
            state = [state[h] * keep[i] + grow[i] for h, i in zip(heads, idx)]
        for h in heads:
            st_ref[h] = state[h]
        for i, (u, h) in enumerate(chains):
            o = outs[i]
            on = o * lax.rsqrt(jnp.mean(o * o, axis=-1, keepdims=True) + RMS_EPS) * on_ref[...]
            zg = z_ref[rows[u], vcs[h]].astype(F32)
            o_ref[rows[u], vcs[h]] = (on * (zg * _sigmoid(zg))).astype(BF16)
        return carry

    lax.fori_loop(0, rblk // (nu * c), chunks, 0)


def _gla_call(proj3, small3, w2p, gla_b, onorm, layer, *, rblk=1024):
    b, seq, _ = proj3.shape
    rblk = min(rblk, seq)
    wk = N_HEADS * HEAD_DIM
    wv = N_HEADS * GLA_VD
    return pl.pallas_call(
        _gla_kernel,
        out_shape=jax.ShapeDtypeStruct((b, seq, wv), BF16),
        grid=(b, seq // rblk),
        in_specs=[
            pl.BlockSpec((None, rblk, wk), lambda bi, i: (bi, i, OFF_LQ // wk)),
            pl.BlockSpec((None, rblk, wk), lambda bi, i: (bi, i, OFF_LK // wk)),
            pl.BlockSpec((None, rblk, wv), lambda bi, i: (bi, i, OFF_LV // wv)),
            pl.BlockSpec((None, rblk, wv), lambda bi, i: (bi, i, OFF_LZ // wv)),
            pl.BlockSpec((None, rblk, LANES), lambda bi, i: (bi, i, 0)),
            pl.BlockSpec((None, LANES, wk), lambda bi, i: (layer, 0, 0)),
            pl.BlockSpec((None, 1, wk), lambda bi, i: (layer, 0, 0)),
            pl.BlockSpec((None, 1, GLA_VD), lambda bi, i: (layer, 0, 0)),
        ],
        out_specs=pl.BlockSpec((None, rblk, wv), lambda bi, i: (bi, i, 0)),
        scratch_shapes=[pltpu.VMEM((N_HEADS, GLA_VD, HEAD_DIM), F32)],
        compiler_params=pltpu.CompilerParams(
            dimension_semantics=("arbitrary", "arbitrary"), vmem_limit_bytes=VMEM_LIMIT),
        name="gla",
    )(proj3, proj3, proj3, proj3, small3, w2p, gla_b, onorm)


def _out_kernel(ysb_ref, ygdn_ref, ygla_ref, mg_ref, mb_ref, psb_ref, pgdn_ref, pgla_ref,
                wout_ref, x_ref, mod_ref, o_ref):
    d = x_ref.shape[1]
    gates = _sigmoid(mg_ref[...].astype(F32) + mb_ref[...])
    merged = gates[:, 0:d] * _dot(ysb_ref[...], psb_ref[...])
    merged = merged + gates[:, d:2 * d] * _dot(ygdn_ref[...], pgdn_ref[...])
    merged = merged + gates[:, 2 * d:3 * d] * _dot(ygla_ref[...], pgla_ref[...])
    o_ref[...] = x_ref[...] + mod_ref[:, 2 * d:3 * d] * _dot(merged.astype(BF16), wout_ref[...])


def _out_call(ysb, ygdn, ygla, proj, merge_b, psb, pgdn, pgla, wout, x2, mod, layer, seq, *, tm):
    t, d = x2.shape
    tm = min(tm, seq)
    rows_per_seq = seq // tm
    wsb, wgdn, wgla = ysb.shape[1], ygdn.shape[1], ygla.shape[1]

    def wspec(k):
        return pl.BlockSpec((None, k, d), lambda i: (layer, 0, 0))

    return pl.pallas_call(
        _out_kernel,
        out_shape=jax.ShapeDtypeStruct((t, d), F32),
        grid=(t // tm,),
        in_specs=[
            pl.BlockSpec((tm, wsb), lambda i: (i, 0)),
            pl.BlockSpec((tm, wgdn), lambda i: (i, 0)),
            pl.BlockSpec((tm, wgla), lambda i: (i, 0)),
            pl.BlockSpec((tm, N_BRANCH * d), lambda i: (i, OFF_MG // (N_BRANCH * d))),
            pl.BlockSpec((None, 1, N_BRANCH * d), lambda i: (layer, 0, 0)),
            wspec(wsb), wspec(wgdn), wspec(wgla), wspec(d),
            pl.BlockSpec((tm, d), lambda i: (i, 0)),
            pl.BlockSpec((None, None, 1, 3 * d), lambda i: (layer, i // rows_per_seq, 0, 0)),
        ],
        out_specs=pl.BlockSpec((tm, d), lambda i: (i, 0)),
        compiler_params=pltpu.CompilerParams(
            dimension_semantics=("arbitrary",), vmem_limit_bytes=VMEM_LIMIT),
        name="merge_out",
    )(ysb, ygdn, ygla, proj, merge_b, psb, pgdn, pgla, wout, x2, mod)


def _reorder_w_in(w_in):
    hw = N_HEADS * HEAD_DIM
    sb = w_in[..., 0:4 * hw]
    dqkv = w_in[..., 4 * hw:7 * hw]
    dz = w_in[..., 7 * hw:8 * hw]
    o = 8 * hw
    db = w_in[..., o:o + N_HEADS]
    da = w_in[..., o + N_HEADS:o + 2 * N_HEADS]
    o += 2 * N_HEADS
    lqk = w_in[..., o:o + 2 * hw]
    lvz = w_in[..., o + 2 * hw:o + 2 * hw + 2 * N_HEADS * GLA_VD]
    o += 2 * hw + 2 * N_HEADS * GLA_VD
    lr = w_in[..., o:o + GLA_RANK]
    mg = w_in[..., o + GLA_RANK:]
    main = jnp.concatenate([mg, dqkv, dz, sb, lqk, lvz], axis=-1).astype(BF16)
    pad = jnp.zeros(w_in.shape[:-1] + (LANES - GLA_RANK - 2 * N_HEADS,), w_in.dtype)
    small = jnp.concatenate([lr, db, da, pad], axis=-1).astype(BF16)
    return main, small


def kernel(x, c, ada_w, ada_b, norm_g, w_in, sb_qnorm, sb_knorm, gdn_conv, gdn_a_log, gdn_dt_bias,
           gdn_onorm, gla_w2, gla_b, gla_onorm, merge_b, proj_sb, proj_gdn, proj_gla, w_out):
    b, seq, d = x.shape
    n_l = w_in.shape[0]
    t = b * seq

    w_main, w_small = _reorder_w_in(w_in)
    assert w_main.shape[-1] == N_MAIN
    mod = _ada_call(c, ada_w, ada_b).reshape(n_l, b, 1, 3 * d)
    hp = jnp.pad(jnp.stack([gdn_a_log, gdn_dt_bias], axis=1),
                 ((0, 0), (0, 0), (SM_DA, LANES - SM_DA - N_HEADS)))
    w2p = jnp.pad(gla_w2, ((0, 0), (SM_LR, LANES - SM_LR - GLA_RANK), (0, 0))).astype(BF16)
    r3 = lambda a: a.reshape(n_l, 1, a.shape[-1])
    psb, pgdn, pgla, wout = (a.astype(BF16) for a in (proj_sb, proj_gdn, proj_gla, w_out))

    x2 = x.reshape(t, d)
    for layer in range(n_l):
        proj, small = _in_call(x2, mod, r3(norm_g), w_main, w_small, layer, seq, tm=1024, tn=2560)
        proj3 = proj.reshape(b, seq, N_MAIN)
        small3 = small.reshape(b, seq, LANES)
        ysb = _sb_call(proj3, r3(sb_qnorm), r3(sb_knorm), layer)
        ygdn = _gdn_call(proj3, small3, gdn_conv, hp, r3(gdn_onorm), layer)
        ygla = _gla_call(proj3, small3, w2p, r3(gla_b), r3(gla_onorm), layer)
        x2 = _out_call(ysb.reshape(t, -1), ygdn.reshape(t, -1), ygla.reshape(t, -1), proj,
                       r3(merge_b), psb, pgdn, pgla, wout, x2, mod, layer, seq, tm=512)
    return x2.reshape(b, seq, d)
```

```python
import functools
from typing import NamedTuple

import jax
import jax.numpy as jnp
from jax import lax
from jax.experimental import pallas as pl
from jax.experimental.pallas import tpu as pltpu

F32 = jnp.float32
BF16 = jnp.bfloat16

N_HEADS = 4
HEAD_DIM = 128
GLA_VD = 256
GDN_CONV = 4
GLA_RANK = 16
GLA_TAU = 16.0
N_BRANCH = 3
RMS_EPS = 1e-6
L2_EPS = 1e-6
CHUNK = 64
GDN_UNROLL = 8
GLA_UNROLL = 4
SUB = 16
LANES = 128
LOG2E = 1.4426950408889634
SB_DEAD = -160.0
INV_BASE = 8
V7X_VMEM_BYTES = 64 * 1024 * 1024
VMEM_LIMIT = V7X_VMEM_BYTES - 8 * 1024 * 1024

OFF_MG = 0
OFF_DQKV = 3072
OFF_DZ = 4608
OFF_SQ, OFF_SK, OFF_SV, OFF_SZ = 5120, 5632, 6144, 6656
OFF_LQ, OFF_LK, OFF_LV, OFF_LZ = 7168, 7680, 8192, 9216
N_MAIN = 10240
SM_LR, SM_DB, SM_DA = 0, 16, 20


class _Tiles(NamedTuple):
    in_rows: int
    in_cols: int
    out_rows: int
    sb_rows: int
    sb_seqs: int
    lin_rows: int
    ada_cols: int


def _tiles(batch, seq, d_model):
    t = _Tiles(in_rows=min(1024, seq), in_cols=N_MAIN // 4, out_rows=min(512, seq), sb_rows=2 * LANES,
               sb_seqs=min(2, batch), lin_rows=min(1024, seq), ada_cols=3 * d_model // 4)
    assert seq % t.in_rows == 0 and seq % t.out_rows == 0 and seq % t.sb_rows == 0
    assert batch % t.sb_seqs == 0 and seq % t.lin_rows == 0
    assert t.lin_rows % (GDN_UNROLL * CHUNK) == 0 and t.lin_rows % (GLA_UNROLL * CHUNK) == 0
    assert t.in_cols % LANES == 0 and t.ada_cols % LANES == 0
    return t


def _dot(a, b):
    return jnp.dot(a, b, preferred_element_type=F32)


def _dot_nt(a, b):
    return lax.dot_general(a, b, (((1,), (1,)), ((), ())), preferred_element_type=F32)


def _dot_tn(a, b):
    return lax.dot_general(a, b, (((0,), (0,)), ((), ())), preferred_element_type=F32)


def _sigmoid(x):
    return 1.0 / (1.0 + jnp.exp2(x * -LOG2E))


def _softplus(x):
    return jnp.maximum(x, 0.0) + jnp.log(1.0 + jnp.exp(-jnp.abs(x)))


def _iota(shape, axis):
    return lax.broadcasted_iota(jnp.int32, shape, axis)


def _ada_kernel(c_ref, w_ref, b_ref, o_ref):
    c = c_ref[...]
    ca = c * _sigmoid(c)
    o_ref[...] = jnp.dot(ca, w_ref[...], preferred_element_type=F32,
                         precision=lax.Precision.HIGHEST) + b_ref[...]


def _ada_call(c, ada_w, ada_b, *, tn):
    n_l, d, d3 = ada_w.shape
    b = c.shape[0]
    return pl.pallas_call(
        _ada_kernel,
        out_shape=jax.ShapeDtypeStruct((n_l, b, d3), F32),
        grid=(n_l, d3 // tn),
        in_specs=[
            pl.BlockSpec((b, d), lambda l, j: (0, 0)),
            pl.BlockSpec((None, d, tn), lambda l, j: (l, 0, j)),
            pl.BlockSpec((None, 1, tn), lambda l, j: (l, 0, j)),
        ],
        out_specs=pl.BlockSpec((None, b, tn), lambda l, j: (l, 0, j)),
        compiler_params=pltpu.CompilerParams(
            dimension_semantics=("arbitrary", "arbitrary"), vmem_limit_bytes=VMEM_LIMIT),
        name="adaln_mod",
    )(c, ada_w, ada_b.reshape(n_l, 1, d3))


def _in_kernel(x_ref, mod_ref, g_ref, w_ref, ws_ref, out_ref, small_ref, h_ref, *, slab):
    tm, d = x_ref.shape

    @pl.when(pl.program_id(1) == 0)
    def _():
        shift = mod_ref[:, 0:d]
        scale1 = 1.0 + mod_ref[:, d:2 * d]
        gain = g_ref[...]

        def body(s, carry):
            rows = pl.ds(pl.multiple_of(s * slab, slab), slab)
            x = x_ref[rows, :]
            y = x * lax.rsqrt(jnp.mean(x * x, axis=-1, keepdims=True) + RMS_EPS) * gain
            h_ref[rows, :] = (y * scale1 + shift).astype(BF16)
            return carry

        lax.fori_loop(0, tm // slab, body, 0)
        small_ref[...] = _dot(h_ref[...], ws_ref[...])

    out_ref[...] = _dot(h_ref[...], w_ref[...]).astype(BF16)


def _in_call(x2, mod, norm_g, w_main, w_small, layer, seq, *, tm, tn):
    t, d = x2.shape
    n_main = w_main.shape[-1]
    rows_per_seq = seq // tm
    kern = functools.partial(_in_kernel, slab=min(128, tm))
    return pl.pallas_call(
        kern,
        out_shape=(jax.ShapeDtypeStruct((t, n_main), BF16),
                   jax.ShapeDtypeStruct((t, LANES), F32)),
        grid=(t // tm, n_main // tn),
        in_specs=[
            pl.BlockSpec((tm, d), lambda i, j: (i, 0)),
            pl.BlockSpec((None, None, 1, 3 * d), lambda i, j: (layer, i // rows_per_seq, 0, 0)),
            pl.BlockSpec((None, 1, d), lambda i, j: (layer, 0, 0)),
            pl.BlockSpec((None, d, tn), lambda i, j: (layer, 0, j)),
            pl.BlockSpec((None, d, LANES), lambda i, j: (layer, 0, 0)),
        ],
        out_specs=(pl.BlockSpec((tm, tn), lambda i, j: (i, j)),
                   pl.BlockSpec((tm, LANES), lambda i, j: (i, 0))),
        scratch_shapes=[pltpu.VMEM((tm, d), BF16)],
        compiler_params=pltpu.CompilerParams(
            dimension_semantics=("arbitrary", "arbitrary"), vmem_limit_bytes=VMEM_LIMIT),
        name="in_proj",
    )(x2, mod, norm_g, w_main, w_small)


def _sb_kernel(q_ref, k_ref, v_ref, z_ref, qw_ref, kw_ref, o_ref, kn_ref, qn_ref, acc_ref, run_ref,
               *, tq):
    nb, seq = q_ref.shape[0], q_ref.shape[1]
    hd = HEAD_DIM
    scale = hd ** -0.5
    heads = range(N_HEADS)
    hcs = [slice(h * hd, (h + 1) * hd) for h in heads]
    chains = [(b, h, b * N_HEADS + h) for b in range(nb) for h in heads]

    def rms(x, w):
        return x * lax.rsqrt(jnp.mean(x * x, axis=-1, keepdims=True) + RMS_EPS) * w

    def knorm(s, carry):
        rows = pl.ds(pl.multiple_of(s * tq, tq), tq)
        for b, h, _ in chains:
            kn_ref[b, rows, hcs[h]] = rms(k_ref[b, rows, hcs[h]].astype(F32), kw_ref[...]).astype(BF16)
        return carry

    lax.fori_loop(0, seq // tq, knorm, 0)

    later = jnp.where(_iota((tq, tq), 0) > _iota((tq, tq), 1), 1.0, 0.0).astype(BF16)
    half = tq // 2
    mask_top = _iota((half, half), 1) < _iota((half, half), 0)
    mask_bot = _iota((half, tq), 1) < half + _iota((half, tq), 0)

    def run_jobs(jobs, first):
        z = [_dot_nt(qn_ref[s, rs], kn_ref[b, keys, hcs[h]]) for (b, h, s), rs, keys, nk, m in jobs]
        t = [jnp.log2(1.0 + jnp.exp2(-jnp.abs(x))) for x in z]
        lb = [jnp.minimum(x, 0.0) - y for x, y in zip(z, t)]
        lg = [x - y for x, y in zip(lb, z)]
        lg = [x if job[4] is None else jnp.where(job[4], x, 0.0) for x, job in zip(lg, jobs)]
        tail = [_dot(x.astype(BF16), later[:job[3], :job[3]]) for x, job in zip(lg, jobs)]
        ws = []
        for ((b, h, s), rs, keys, nk, m), lbi, lgi, ti in zip(jobs, lb, lg, tail):
            tot = jnp.sum(lgi, axis=1, keepdims=True)
            if first:
                w = jnp.exp2(lbi + ti)
                run_ref[s, rs] = tot
            else:
                run = run_ref[s, rs]
                w = jnp.exp2(lbi + ti + run)
                run_ref[s, rs] = run + tot
            if m is not None:
                w = jnp.where(m, w, 0.0)
            ws.append(w.astype(BF16))
        for ((b, h, s), rs, keys, nk, m), w in zip(jobs, ws):
            pv = _dot(w, v_ref[b, keys, hcs[h]])
            if first:
                acc_ref[s, rs] = pv
            else:
                acc_ref[s, rs] += pv

    def q_block(i, carry):
        r0 = pl.multiple_of(i * tq, tq)
        rows = pl.ds(r0, tq)
        for b, h, s in chains:
            qn_ref[s] = (rms(q_ref[b, rows, hcs[h]].astype(F32), qw_ref[...]) * (scale * LOG2E)).astype(BF16)
        run_jobs([(ch, slice(0, half), pl.ds(r0, half), half, mask_top) for ch in chains]
                 + [(ch, slice(half, tq), pl.ds(r0, tq), tq, mask_bot) for ch in chains], True)

        def alive():
            return jnp.max(run_ref[...]) > SB_DEAD

        def below(carry):
            n, _ = carry
            keys = pl.ds(pl.multiple_of((i - 1 - n) * tq, tq), tq)
            run_jobs([(ch, slice(0, tq), keys, tq, None) for ch in chains], False)
            return n + 1, alive()

        lax.while_loop(lambda carry: (carry[0] < i) & carry[1], below, (0, alive()))
        for b, h, s in chains:
            zg = z_ref[b, rows, hcs[h]].astype(F32)
            o_ref[b, rows, hcs[h]] = (acc_ref[s] * (zg * _sigmoid(zg))).astype(BF16)
        return carry

    lax.fori_loop(0, seq // tq, q_block, 0)


def _sb_call(proj3, qw, kw, layer, *, tq, nb):
    b, seq, _ = proj3.shape
    hd = HEAD_DIM
    w = N_HEADS * hd

    def col(off):
        return pl.BlockSpec((nb, seq, w), lambda bi: (bi, 0, off // w))

    wspec = pl.BlockSpec((None, 1, hd), lambda bi: (layer, 0, 0))
    return pl.pallas_call(
        functools.partial(_sb_kernel, tq=tq),
        out_shape=jax.ShapeDtypeStruct((b, seq, w), BF16),
        grid=(b // nb,),
        in_specs=[col(OFF_SQ), col(OFF_SK), col(OFF_SV), col(OFF_SZ), wspec, wspec],
        out_specs=pl.BlockSpec((nb, seq, w), lambda bi: (bi, 0, 0)),
        scratch_shapes=[pltpu.VMEM((nb, seq, w), BF16),
                        pltpu.VMEM((nb * N_HEADS, tq, hd), BF16),
                        pltpu.VMEM((nb * N_HEADS, tq, hd), F32),
                        pltpu.VMEM((nb * N_HEADS, tq, 1), F32)],
        compiler_params=pltpu.CompilerParams(
            dimension_semantics=("arbitrary",), vmem_limit_bytes=VMEM_LIMIT),
        name="sb_attn",
    )(proj3, proj3, proj3, proj3, qw, kw)


def _unit_lower_inverse(a_list, ri, ci):
    size = ri.shape[0]
    same = lambda w: (ri // w) == (ci // w)
    eye = jnp.where(ri == ci, 1.0, 0.0).astype(F32)
    d = [jnp.where(same(INV_BASE), a, 0.0) for a in a_list]
    x = [eye - y for y in d]
    p = d
    w = 2
    while w < INV_BASE:
        pb = [y.astype(BF16) for y in p]
        p = [_dot(y, y) for y in pb]
        x = [y + _dot(y.astype(BF16), z.astype(BF16)) for y, z in zip(x, p)]
        w *= 2
    w = INV_BASE
    while w < size:
        below = same(2 * w) & jnp.logical_not(same(w))
        xb = [y.astype(BF16) for y in x]
        lx = [_dot(jnp.where(below, a, 0.0).astype(BF16), y) for a, y in zip(a_list, xb)]
        x = [y - _dot(yb, z.astype(BF16)) for y, yb, z in zip(x, xb, lx)]
        w *= 2
    return x


def _gdn_kernel(qkv_ref, prev_ref, dz_ref, sm_ref, conv_ref, hp_ref, on_ref, o_ref,
                xs_ref, st_ref):
    rblk = qkv_ref.shape[0]
    hd = HEAD_DIM
    c = CHUNK
    halo = prev_ref.shape[0]
    blk = pl.program_id(1)

    @pl.when(blk == 0)
    def _():
        st_ref[...] = jnp.zeros_like(st_ref)
        xs_ref[0:halo, :] = jnp.zeros((halo, xs_ref.shape[1]), F32)

    @pl.when(blk > 0)
    def _():
        xs_ref[0:halo, :] = prev_ref[...].astype(F32)

    def stage(s, carry):
        rows = pl.multiple_of(s * c, c)
        xs_ref[pl.ds(halo + rows, c), :] = qkv_ref[pl.ds(rows, c), :].astype(F32)
        return carry

    lax.fori_loop(0, rblk // c, stage, 0)

    ri = _iota((c, c), 0)
    ci = _iota((c, c), 1)
    tri_incl = ri >= ci
    tri_strict = ri > ci
    tri = jnp.where(tri_incl, 1.0, 0.0).astype(BF16)
    later = jnp.where(_iota((c, LANES), 0) > _iota((c, LANES), 1), 1.0, 0.0).astype(F32)

    nu = GDN_UNROLL
    heads = range(N_HEADS)
    hcs = [slice(h * hd, (h + 1) * hd) for h in heads]
    chains = [(u, h) for u in range(nu) for h in heads]

    pre = 8

    def conv(win, col0):
        w = xs_ref[win, col0:col0 + hd]
        cw = conv_ref[:, col0:col0 + hd]
        y = w[pre:] * cw[GDN_CONV - 1:GDN_CONV]
        for j in range(1, GDN_CONV):
            y = y + pltpu.roll(w, j, axis=0)[pre:] * cw[GDN_CONV - 1 - j:GDN_CONV - j]
        return y * _sigmoid(y)

    def chunks(n, carry):
        base = n * (nu * c)
        r0 = [pl.multiple_of(base + u * c, c) for u in range(nu)]
        rows = [pl.ds(r0[u], c) for u in range(nu)]
        win = [pl.ds(pl.multiple_of(r0[u] + (halo - pre), 8), c + pre) for u in range(nu)]
        sm = [sm_ref[rows[u], :] for u in range(nu)]

        q = [conv(win[u], h * hd) for u, h in chains]
        k = [conv(win[u], N_HEADS * hd + h * hd) for u, h in chains]
        v = [conv(win[u], 2 * N_HEADS * hd + h * hd) for u, h in chains]
        q = [x * (lax.rsqrt(jnp.sum(x * x, axis=-1, keepdims=True) + L2_EPS) * (hd ** -0.5))
             for x in q]
        k = [x * lax.rsqrt(jnp.sum(x * x, axis=-1, keepdims=True) + L2_EPS) for x in k]
        bt = [_sigmoid(x) for x in sm]
        gt = [-jnp.exp(hp_ref[0:1, :]) * _softplus(x + hp_ref[1:2, :]) for x in sm]
        beta = [jnp.broadcast_to(bt[u][:, SM_DB + h:SM_DB + h + 1], (c, hd)) for u, h in chains]
        g = [jnp.broadcast_to(gt[u][:, SM_DA + h:SM_DA + h + 1], (c, hd))
             for u, h in chains]

        cum = [_dot(tri, jnp.concatenate([x, x * later], axis=1).astype(BF16)) for x in g]
        gam = [x[:, :hd] for x in cum]
        decay = [jnp.where(tri_incl, jnp.exp(x[:, hd:hd + c]), 0.0) for x in cum]
        eg = [jnp.exp(x) for x in gam]
        gl = [x[c - 1:c, :] for x in gam]

        kb = [x.astype(BF16) for x in k]
        kk = [_dot_nt(x, x) for x in kb]
        qk = [(_dot_nt(q[i].astype(BF16), kb[i]) * decay[i]).astype(BF16) for i in range(len(chains))]
        a = [jnp.where(tri_strict, beta[i][:, :c] * kk[i] * decay[i], 0.0) for i in range(len(chains))]
        tinv = _unit_lower_inverse(a, ri, ci)
        sol = [_dot(tinv[i].astype(BF16),
                    jnp.concatenate([v[i] * beta[i], k[i] * (beta[i] * eg[i])], axis=1).astype(BF16))
               for i in range(len(chains))]
        lhs = [jnp.concatenate([sol[i][:, hd:], q[i] * eg[i]], axis=0).astype(BF16)
               for i in range(len(chains))]
        kd = [(k[i] * jnp.exp(gl[i] - gam[i])).astype(BF16) for i in range(len(chains))]
        cd = [jnp.exp(x) for x in gl]

        state = [st_ref[h] for h in heads]
        outs = []
        for u in range(nu):
            idx = [u * N_HEADS + h for h in heads]
            r = [_dot(lhs[i], state[h].astype(BF16)) for h, i in zip(heads, idx)]
            ub = [(sol[i][:, :hd] - r[h][:c]).astype(BF16) for h, i in zip(heads, idx)]
            outs.extend(r[h][c:] + _dot(qk[i], ub[h]) for h, i in zip(heads, idx))
            state = [state[h] * cd[i] + _dot_tn(kd[i], ub[h]) for h, i in zip(heads, idx)]
        for h in heads:
            st_ref[h] = state[h]
        for i, (u, h) in enumerate(chains):
            o = outs[i]
            on = o * lax.rsqrt(jnp.mean(o * o, axis=-1, keepdims=True) + RMS_EPS) * on_ref[...]
            zg = dz_ref[rows[u], hcs[h]].astype(F32)
            o_ref[rows[u], hcs[h]] = (on * (zg * _sigmoid(zg))).astype(BF16)
        return carry

    lax.fori_loop(0, rblk // (nu * c), chunks, 0)


def _gdn_call(proj3, small3, conv_w, hp, onorm, layer, *, rblk, halo=16):
    b, seq, _ = proj3.shape
    w3 = 3 * N_HEADS * HEAD_DIM
    wz = N_HEADS * HEAD_DIM
    per = rblk // halo
    return pl.pallas_call(
        _gdn_kernel,
        out_shape=jax.ShapeDtypeStruct((b, seq, wz), BF16),
        grid=(b, seq // rblk),
        in_specs=[
            pl.BlockSpec((None, rblk, w3), lambda bi, i: (bi, i, OFF_DQKV // w3)),
            pl.BlockSpec((None, halo, w3),
                         lambda bi, i: (bi, jnp.maximum(i * per - 1, 0), OFF_DQKV // w3)),
            pl.BlockSpec((None, rblk, wz), lambda bi, i: (bi, i, OFF_DZ // wz)),
            pl.BlockSpec((None, rblk, LANES), lambda bi, i: (bi, i, 0)),
            pl.BlockSpec((None, GDN_CONV, w3), lambda bi, i: (layer, 0, 0)),
            pl.BlockSpec((None, 2, LANES), lambda bi, i: (layer, 0, 0)),
            pl.BlockSpec((None, 1, HEAD_DIM), lambda bi, i: (layer, 0, 0)),
        ],
        out_specs=pl.BlockSpec((None, rblk, wz), lambda bi, i: (bi, i, 0)),
        scratch_shapes=[pltpu.VMEM((rblk + halo, w3), F32),
                        pltpu.VMEM((N_HEADS, HEAD_DIM, HEAD_DIM), F32)],
        compiler_params=pltpu.CompilerParams(
            dimension_semantics=("arbitrary", "arbitrary"), vmem_limit_bytes=VMEM_LIMIT),
        name="gdn",
    )(proj3, proj3, proj3, small3, conv_w, hp, onorm)


def _gla_kernel(q_ref, k_ref, v_ref, z_ref, sm_ref, w2_ref, b_ref, on_ref, o_ref, st_ref):
    rblk = q_ref.shape[0]
    hd = HEAD_DIM
    vd = GLA_VD
    c = CHUNK

    @pl.when(pl.program_id(1) == 0)
    def _():
        st_ref[...] = jnp.zeros_like(st_ref)

    tri_incl = _iota((c, c), 0) >= _iota((c, c), 1)
    tri = jnp.where(tri_incl, 1.0, 0.0).astype(BF16)
    nu = GLA_UNROLL
    heads = range(N_HEADS)
    hcs = [slice(h * hd, (h + 1) * hd) for h in heads]
    vcs = [slice(h * vd, (h + 1) * vd) for h in heads]
    chains = [(u, h) for u in range(nu) for h in heads]
    nc = len(chains)

    def chunks(n, carry):
        base = n * (nu * c)
        rows = [pl.ds(pl.multiple_of(base + u * c, c), c) for u in range(nu)]
        la = []
        for u in range(nu):
            x = _dot(sm_ref[rows[u], :].astype(BF16), w2_ref[...]) + b_ref[...]
            la.append(-_softplus(-x) * (1.0 / GLA_TAU))
        gc = [_dot(tri, la[u][:, hcs[h]].astype(BF16)) for u, h in chains]
        gl = [x[c - 1:c, :] for x in gc]
        q = [q_ref[rows[u], hcs[h]].astype(F32) * (hd ** -0.5) for u, h in chains]
        k = [k_ref[rows[u], hcs[h]].astype(F32) for u, h in chains]
        v = [v_ref[rows[u], vcs[h]] for u, h in chains]
        qd = [(q[i] * jnp.exp(gc[i])).astype(BF16) for i in range(nc)]
        att = []
        for i in range(nc):
            parts = []
            for j in range(c // SUB):
                rs = slice(j * SUB, (j + 1) * SUB)
                ref_g = gc[i][j * SUB:j * SUB + 1, :]
                qi = (q[i][rs] * jnp.exp(gc[i][rs] - ref_g)).astype(BF16)
                ki = (k[i] * jnp.exp(jnp.minimum(ref_g - gc[i], 60.0))).astype(BF16)
                parts.append(_dot_nt(qi, ki))
            att.append(jnp.where(tri_incl, jnp.concatenate(parts, axis=0), 0.0).astype(BF16))
        intra = [_dot(att[i], v[i]) for i in range(nc)]
        grow = [_dot_tn(v[i], (k[i] * jnp.exp(gl[i] - gc[i])).astype(BF16)) for i in range(nc)]
        keep = [jnp.exp(x) for x in gl]

        state = [st_ref[h] for h in heads]
        outs = []
        for u in range(nu):
            idx = [u * N_HEADS + h for h in heads]
            outs.extend(intra[i] + _dot_nt(qd[i], state[h].astype(BF16)) for h, i in zip(heads, idx))
            state = [state[h] * keep[i] + grow[i] for h, i in zip(heads, idx)]
        for h in heads:
            st_ref[h] = state[h]
        for i, (u, h) in enumerate(chains):
            o = outs[i]
            on = o * lax.rsqrt(jnp.mean(o * o, axis=-1, keepdims=True) + RMS_EPS) * on_ref[...]
            zg = z_ref[rows[u], vcs[h]].astype(F32)
            o_ref[rows[u], vcs[h]] = (on * (zg * _sigmoid(zg))).astype(BF16)
        return carry

    lax.fori_loop(0, rblk // (nu * c), chunks, 0)


def _gla_call(proj3, small3, w2p, gla_b, onorm, layer, *, rblk):
    b, seq, _ = proj3.shape
    wk = N_HEADS * HEAD_DIM
    wv = N_HEADS * GLA_VD
    return pl.pallas_call(
        _gla_kernel,
        out_shape=jax.ShapeDtypeStruct((b, seq, wv), BF16),
        grid=(b, seq // rblk),
        in_specs=[
            pl.BlockSpec((None, rblk, wk), lambda bi, i: (bi, i, OFF_LQ // wk)),
            pl.BlockSpec((None, rblk, wk), lambda bi, i: (bi, i, OFF_LK // wk)),
            pl.BlockSpec((None, rblk, wv), lambda bi, i: (bi, i, OFF_LV // wv)),
            pl.BlockSpec((None, rblk, wv), lambda bi, i: (bi, i, OFF_LZ // wv)),
            pl.BlockSpec((None, rblk, LANES), lambda bi, i: (bi, i, 0)),
            pl.BlockSpec((None, LANES, wk), lambda bi, i: (layer, 0, 0)),
            pl.BlockSpec((None, 1, wk), lambda bi, i: (layer, 0, 0)),
            pl.BlockSpec((None, 1, GLA_VD), lambda bi, i: (layer, 0, 0)),
        ],
        out_specs=pl.BlockSpec((None, rblk, wv), lambda bi, i: (bi, i, 0)),
        scratch_shapes=[pltpu.VMEM((N_HEADS, GLA_VD, HEAD_DIM), F32)],
        compiler_params=pltpu.CompilerParams(
            dimension_semantics=("arbitrary", "arbitrary"), vmem_limit_bytes=VMEM_LIMIT),
        name="gla",
    )(proj3, proj3, proj3, proj3, small3, w2p, gla_b, onorm)


def _out_kernel(ysb_ref, ygdn_ref, ygla_ref, mg_ref, mb_ref, psb_ref, pgdn_ref, pgla_ref,
                wout_ref, x_ref, mod_ref, o_ref):
    d = x_ref.shape[1]
    gates = _sigmoid(mg_ref[...].astype(F32) + mb_ref[...])
    merged = gates[:, 0:d] * _dot(ysb_ref[...], psb_ref[...])
    merged = merged + gates[:, d:2 * d] * _dot(ygdn_ref[...], pgdn_ref[...])
    merged = merged + gates[:, 2 * d:3 * d] * _dot(ygla_ref[...], pgla_ref[...])
    o_ref[...] = x_ref[...] + mod_ref[:, 2 * d:3 * d] * _dot(merged.astype(BF16), wout_ref[...])


def _out_call(ysb, ygdn, ygla, proj, merge_b, psb, pgdn, pgla, wout, x2, mod, layer, seq, *, tm):
    t, d = x2.shape
    rows_per_seq = seq // tm
    wsb, wgdn, wgla = ysb.shape[1], ygdn.shape[1], ygla.shape[1]

    def wspec(k):
        return pl.BlockSpec((None, k, d), lambda i: (layer, 0, 0))

    return pl.pallas_call(
        _out_kernel,
        out_shape=jax.ShapeDtypeStruct((t, d), F32),
        grid=(t // tm,),
        in_specs=[
            pl.BlockSpec((tm, wsb), lambda i: (i, 0)),
            pl.BlockSpec((tm, wgdn), lambda i: (i, 0)),
            pl.BlockSpec((tm, wgla), lambda i: (i, 0)),
            pl.BlockSpec((tm, N_BRANCH * d), lambda i: (i, OFF_MG // (N_BRANCH * d))),
            pl.BlockSpec((None, 1, N_BRANCH * d), lambda i: (layer, 0, 0)),
            wspec(wsb), wspec(wgdn), wspec(wgla), wspec(d),
            pl.BlockSpec((tm, d), lambda i: (i, 0)),
            pl.BlockSpec((None, None, 1, 3 * d), lambda i: (layer, i // rows_per_seq, 0, 0)),
        ],
        out_specs=pl.BlockSpec((tm, d), lambda i: (i, 0)),
        compiler_params=pltpu.CompilerParams(
            dimension_semantics=("arbitrary",), vmem_limit_bytes=VMEM_LIMIT),
        name="merge_out",
    )(ysb, ygdn, ygla, proj, merge_b, psb, pgdn, pgla, wout, x2, mod)


def _reorder_w_in(w_in):
    hw = N_HEADS * HEAD_DIM
    sb = w_in[..., 0:4 * hw]
    dqkv = w_in[..., 4 * hw:7 * hw]
    dz = w_in[..., 7 * hw:8 * hw]
    o = 8 * hw
    db = w_in[..., o:o + N_HEADS]
    da = w_in[..., o + N_HEADS:o + 2 * N_HEADS]
    o += 2 * N_HEADS
    lqk = w_in[..., o:o + 2 * hw]
    lvz = w_in[..., o + 2 * hw:o + 2 * hw + 2 * N_HEADS * GLA_VD]
    o += 2 * hw + 2 * N_HEADS * GLA_VD
    lr = w_in[..., o:o + GLA_RANK]
    mg = w_in[..., o + GLA_RANK:]
    main = jnp.concatenate([mg, dqkv, dz, sb, lqk, lvz], axis=-1).astype(BF16)
    pad = jnp.zeros(w_in.shape[:-1] + (LANES - GLA_RANK - 2 * N_HEADS,), w_in.dtype)
    small = jnp.concatenate([lr, db, da, pad], axis=-1).astype(BF16)
    return main, small


def kernel(x, c, ada_w, ada_b, norm_g, w_in, sb_qnorm, sb_knorm, gdn_conv, gdn_a_log, gdn_dt_bias,
           gdn_onorm, gla_w2, gla_b, gla_onorm, merge_b, proj_sb, proj_gdn, proj_gla, w_out):
    b, seq, d = x.shape
    n_l = w_in.shape[0]
    t = b * seq

    tl = _tiles(b, seq, d)
    w_main, w_small = _reorder_w_in(w_in)
    assert w_main.shape[-1] == N_MAIN
    mod = _ada_call(c, ada_w, ada_b, tn=tl.ada_cols).reshape(n_l, b, 1, 3 * d)
    hp = jnp.pad(jnp.stack([gdn_a_log, gdn_dt_bias], axis=1),
                 ((0, 0), (0, 0), (SM_DA, LANES - SM_DA - N_HEADS)))
    w2p = jnp.pad(gla_w2, ((0, 0), (SM_LR, LANES - SM_LR - GLA_RANK), (0, 0))).astype(BF16)
    r3 = lambda a: a.reshape(n_l, 1, a.shape[-1])
    psb, pgdn, pgla, wout = (a.astype(BF16) for a in (proj_sb, proj_gdn, proj_gla, w_out))

    x2 = x.reshape(t, d)
    for layer in range(n_l):
        proj, small = _in_call(x2, mod, r3(norm_g), w_main, w_small, layer, seq,
                               tm=tl.in_rows, tn=tl.in_cols)
        proj3 = proj.reshape(b, seq, N_MAIN)
        small3 = small.reshape(b, seq, LANES)
        ysb = _sb_call(proj3, r3(sb_qnorm), r3(sb_knorm), layer, tq=tl.sb_rows, nb=tl.sb_seqs)
        ygdn = _gdn_call(proj3, small3, gdn_conv, hp, r3(gdn_onorm), layer, rblk=tl.lin_rows)
        ygla = _gla_call(proj3, small3, w2p, r3(gla_b), r3(gla_onorm), layer, rblk=tl.lin_rows)
        x2 = _out_call(ysb.reshape(t, -1), ygdn.reshape(t, -1), ygla.reshape(t, -1), proj,
                       r3(merge_b), psb, pgdn, pgla, wout, x2, mod, layer, seq, tm=tl.out_rows)
    return x2.reshape(b, seq, d)
```

```python
import functools
from typing import NamedTuple

import jax
import jax.numpy as jnp
from jax import lax
from jax.experimental import pallas as pl
from jax.experimental.pallas import tpu as pltpu

F32 = jnp.float32
BF16 = jnp.bfloat16

N_HEADS = 4
HEAD_DIM = 128
GLA_VD = 256
GDN_CONV = 4
GLA_RANK = 16
GLA_TAU = 16.0
N_BRANCH = 3
RMS_EPS = 1e-6
L2_EPS = 1e-6
CHUNK = 64
GDN_UNROLL = 8
GLA_UNROLL = 4
SUB = 16
LANES = 128
LOG2E = 1.4426950408889634
SB_DEAD = -160.0
INV_BASE = 8
V7X_VMEM_BYTES = 64 * 1024 * 1024
VMEM_LIMIT = V7X_VMEM_BYTES - 8 * 1024 * 1024

OFF_MG = 0
OFF_DQKV = 3072
OFF_DZ = 4608
OFF_SQ, OFF_SK, OFF_SV, OFF_SZ = 5120, 5632, 6144, 6656
OFF_LQ, OFF_LK, OFF_LV, OFF_LZ = 7168, 7680, 8192, 9216
N_MAIN = 10240
SM_LR, SM_DB, SM_DA = 0, 16, 20


class _Tiles(NamedTuple):
    in_rows: int
    in_cols: int
    out_rows: int
    sb_rows: int
    sb_seqs: int
    lin_rows: int
    ada_cols: int


def _tiles(batch, seq, d_model):
    t = _Tiles(in_rows=min(512, seq), in_cols=N_MAIN // 2, out_rows=min(512, seq), sb_rows=2 * LANES,
               sb_seqs=min(2, batch), lin_rows=min(1024, seq), ada_cols=3 * d_model // 4)
    assert seq % t.in_rows == 0 and seq % t.out_rows == 0 and seq % t.sb_rows == 0
    assert batch % t.sb_seqs == 0 and seq % t.lin_rows == 0
    assert t.lin_rows % (GDN_UNROLL * CHUNK) == 0 and t.lin_rows % (GLA_UNROLL * CHUNK) == 0
    assert t.in_cols % LANES == 0 and t.ada_cols % LANES == 0
    return t


def _dot(a, b):
    return jnp.dot(a, b, preferred_element_type=F32)


def _dot_nt(a, b):
    return lax.dot_general(a, b, (((1,), (1,)), ((), ())), preferred_element_type=F32)


def _dot_tn(a, b):
    return lax.dot_general(a, b, (((0,), (0,)), ((), ())), preferred_element_type=F32)


def _sigmoid(x):
    return 1.0 / (1.0 + jnp.exp2(x * -LOG2E))


def _softplus(x):
    return jnp.maximum(x, 0.0) + jnp.log(1.0 + jnp.exp(-jnp.abs(x)))


def _iota(shape, axis):
    return lax.broadcasted_iota(jnp.int32, shape, axis)


def _ada_kernel(c_ref, w_ref, b_ref, o_ref):
    c = c_ref[...]
    ca = c * _sigmoid(c)
    o_ref[...] = jnp.dot(ca, w_ref[...], preferred_element_type=F32,
                         precision=lax.Precision.HIGHEST) + b_ref[...]


def _ada_call(c, ada_w, ada_b, *, tn):
    n_l, d, d3 = ada_w.shape
    b = c.shape[0]
    return pl.pallas_call(
        _ada_kernel,
        out_shape=jax.ShapeDtypeStruct((n_l, b, d3), F32),
        grid=(n_l, d3 // tn),
        in_specs=[
            pl.BlockSpec((b, d), lambda l, j: (0, 0)),
            pl.BlockSpec((None, d, tn), lambda l, j: (l, 0, j)),
            pl.BlockSpec((None, 1, tn), lambda l, j: (l, 0, j)),
        ],
        out_specs=pl.BlockSpec((None, b, tn), lambda l, j: (l, 0, j)),
        compiler_params=pltpu.CompilerParams(
            dimension_semantics=("arbitrary", "arbitrary"), vmem_limit_bytes=VMEM_LIMIT),
        name="adaln_mod",
    )(c, ada_w, ada_b.reshape(n_l, 1, d3))


def _in_kernel(x_ref, mod_ref, g_ref, w_ref, ws_ref, out_ref, small_ref, h_ref, *, slab):
    tm, d = x_ref.shape

    @pl.when(pl.program_id(1) == 0)
    def _():
        shift = mod_ref[:, 0:d]
        scale1 = 1.0 + mod_ref[:, d:2 * d]
        gain = g_ref[...]

        def body(s, carry):
            rows = pl.ds(pl.multiple_of(s * slab, slab), slab)
            x = x_ref[rows, :]
            y = x * lax.rsqrt(jnp.mean(x * x, axis=-1, keepdims=True) + RMS_EPS) * gain
            h_ref[rows, :] = (y * scale1 + shift).astype(BF16)
            return carry

        lax.fori_loop(0, tm // slab, body, 0)
        small_ref[...] = _dot(h_ref[...], ws_ref[...])

    out_ref[...] = _dot(h_ref[...], w_ref[...]).astype(BF16)


def _in_call(x2, mod, norm_g, w_main, w_small, layer, seq, *, tm, tn):
    t, d = x2.shape
    n_main = w_main.shape[-1]
    rows_per_seq = seq // tm
    kern = functools.partial(_in_kernel, slab=min(128, tm))
    return pl.pallas_call(
        kern,
        out_shape=(jax.ShapeDtypeStruct((t, n_main), BF16),
                   jax.ShapeDtypeStruct((t, LANES), F32)),
        grid=(t // tm, n_main // tn),
        in_specs=[
            pl.BlockSpec((tm, d), lambda i, j: (i, 0)),
            pl.BlockSpec((None, None, 1, 3 * d), lambda i, j: (layer, i // rows_per_seq, 0, 0)),
            pl.BlockSpec((None, 1, d), lambda i, j: (layer, 0, 0)),
            pl.BlockSpec((None, d, tn), lambda i, j: (layer, 0, j)),
            pl.BlockSpec((None, d, LANES), lambda i, j: (layer, 0, 0)),
        ],
        out_specs=(pl.BlockSpec((tm, tn), lambda i, j: (i, j)),
                   pl.BlockSpec((tm, LANES), lambda i, j: (i, 0))),
        scratch_shapes=[pltpu.VMEM((tm, d), BF16)],
        compiler_params=pltpu.CompilerParams(
            dimension_semantics=("arbitrary", "arbitrary"), vmem_limit_bytes=VMEM_LIMIT),
        name="in_proj",
    )(x2, mod, norm_g, w_main, w_small)


def _sb_kernel(q_ref, k_ref, v_ref, z_ref, qw_ref, kw_ref, o_ref, kn_ref, qn_ref, acc_ref, run_ref,
               *, tq):
    nb, seq = q_ref.shape[0], q_ref.shape[1]
    hd = HEAD_DIM
    scale = hd ** -0.5
    heads = range(N_HEADS)
    hcs = [slice(h * hd, (h + 1) * hd) for h in heads]
    chains = [(b, h, b * N_HEADS + h) for b in range(nb) for h in heads]

    def rms(x, w):
        return x * lax.rsqrt(jnp.mean(x * x, axis=-1, keepdims=True) + RMS_EPS) * w

    def knorm(s, carry):
        rows = pl.ds(pl.multiple_of(s * tq, tq), tq)
        for b, h, _ in chains:
            kn_ref[b, rows, hcs[h]] = rms(k_ref[b, rows, hcs[h]].astype(F32), kw_ref[...]).astype(BF16)
        return carry

    lax.fori_loop(0, seq // tq, knorm, 0)

    later = jnp.where(_iota((tq, tq), 0) > _iota((tq, tq), 1), 1.0, 0.0).astype(BF16)
    half = tq // 2
    mask_top = _iota((half, half), 1) < _iota((half, half), 0)
    mask_bot = _iota((half, tq), 1) < half + _iota((half, tq), 0)

    def run_jobs(jobs, first):
        z = [_dot_nt(qn_ref[s, rs], kn_ref[b, keys, hcs[h]]) for (b, h, s), rs, keys, nk, m in jobs]
        t = [jnp.log2(1.0 + jnp.exp2(-jnp.abs(x))) for x in z]
        lb = [jnp.minimum(x, 0.0) - y for x, y in zip(z, t)]
        lg = [x - y for x, y in zip(lb, z)]
        lg = [x if job[4] is None else jnp.where(job[4], x, 0.0) for x, job in zip(lg, jobs)]
        tail = [_dot(x.astype(BF16), later[:job[3], :job[3]]) for x, job in zip(lg, jobs)]
        ws = []
        for ((b, h, s), rs, keys, nk, m), lbi, lgi, ti in zip(jobs, lb, lg, tail):
            tot = jnp.sum(lgi, axis=1, keepdims=True)
            if first:
                w = jnp.exp2(lbi + ti)
                run_ref[s, rs] = tot
            else:
                run = run_ref[s, rs]
                w = jnp.exp2(lbi + ti + run)
                run_ref[s, rs] = run + tot
            if m is not None:
                w = jnp.where(m, w, 0.0)
            ws.append(w.astype(BF16))
        for ((b, h, s), rs, keys, nk, m), w in zip(jobs, ws):
            pv = _dot(w, v_ref[b, keys, hcs[h]])
            if first:
                acc_ref[s, rs] = pv
            else:
                acc_ref[s, rs] += pv

    def q_block(i, carry):
        r0 = pl.multiple_of(i * tq, tq)
        rows = pl.ds(r0, tq)
        for b, h, s in chains:
            qn_ref[s] = (rms(q_ref[b, rows, hcs[h]].astype(F32), qw_ref[...]) * (scale * LOG2E)).astype(BF16)
        run_jobs([(ch, slice(0, half), pl.ds(r0, half), half, mask_top) for ch in chains]
                 + [(ch, slice(half, tq), pl.ds(r0, tq), tq, mask_bot) for ch in chains], True)

        def alive():
            return jnp.max(run_ref[...]) > SB_DEAD

        def below(carry):
            n, _ = carry
            keys = pl.ds(pl.multiple_of((i - 1 - n) * tq, tq), tq)
            run_jobs([(ch, slice(0, tq), keys, tq, None) for ch in chains], False)
            return n + 1, alive()

        lax.while_loop(lambda carry: (carry[0] < i) & carry[1], below, (0, alive()))
        for b, h, s in chains:
            zg = z_ref[b, rows, hcs[h]].astype(F32)
            o_ref[b, rows, hcs[h]] = (acc_ref[s] * (zg * _sigmoid(zg))).astype(BF16)
        return carry

    lax.fori_loop(0, seq // tq, q_block, 0)


def _sb_call(proj3, qw, kw, layer, *, tq, nb):
    b, seq, _ = proj3.shape
    hd = HEAD_DIM
    w = N_HEADS * hd

    def col(off):
        return pl.BlockSpec((nb, seq, w), lambda bi: (bi, 0, off // w))

    wspec = pl.BlockSpec((None, 1, hd), lambda bi: (layer, 0, 0))
    return pl.pallas_call(
        functools.partial(_sb_kernel, tq=tq),
        out_shape=jax.ShapeDtypeStruct((b, seq, w), BF16),
        grid=(b // nb,),
        in_specs=[col(OFF_SQ), col(OFF_SK), col(OFF_SV), col(OFF_SZ), wspec, wspec],
        out_specs=pl.BlockSpec((nb, seq, w), lambda bi: (bi, 0, 0)),
        scratch_shapes=[pltpu.VMEM((nb, seq, w), BF16),
                        pltpu.VMEM((nb * N_HEADS, tq, hd), BF16),
                        pltpu.VMEM((nb * N_HEADS, tq, hd), F32),
                        pltpu.VMEM((nb * N_HEADS, tq, 1), F32)],
        compiler_params=pltpu.CompilerParams(
            dimension_semantics=("arbitrary",), vmem_limit_bytes=VMEM_LIMIT),
        name="sb_attn",
    )(proj3, proj3, proj3, proj3, qw, kw)


def _unit_lower_inverse(a_list, ri, ci):
    size = ri.shape[0]
    same = lambda w: (ri // w) == (ci // w)
    eye = jnp.where(ri == ci, 1.0, 0.0).astype(F32)
    d = [jnp.where(same(INV_BASE), a, 0.0) for a in a_list]
    x = [eye - y for y in d]
    p = d
    w = 2
    while w < INV_BASE:
        pb = [y.astype(BF16) for y in p]
        p = [_dot(y, y) for y in pb]
        x = [y + _dot(y.astype(BF16), z.astype(BF16)) for y, z in zip(x, p)]
        w *= 2
    w = INV_BASE
    while w < size:
        below = same(2 * w) & jnp.logical_not(same(w))
        xb = [y.astype(BF16) for y in x]
        lx = [_dot(jnp.where(below, a, 0.0).astype(BF16), y) for a, y in zip(a_list, xb)]
        x = [y - _dot(yb, z.astype(BF16)) for y, yb, z in zip(x, xb, lx)]
        w *= 2
    return x


def _gdn_kernel(qkv_ref, prev_ref, dz_ref, sm_ref, conv_ref, hp_ref, on_ref, o_ref,
                xs_ref, st_ref):
    rblk = qkv_ref.shape[0]
    hd = HEAD_DIM
    c = CHUNK
    halo = prev_ref.shape[0]
    blk = pl.program_id(1)

    @pl.when(blk == 0)
    def _():
        st_ref[...] = jnp.zeros_like(st_ref)
        xs_ref[0:halo, :] = jnp.zeros((halo, xs_ref.shape[1]), F32)

    @pl.when(blk > 0)
    def _():
        xs_ref[0:halo, :] = prev_ref[...].astype(F32)

    def stage(s, carry):
        rows = pl.multiple_of(s * c, c)
        xs_ref[pl.ds(halo + rows, c), :] = qkv_ref[pl.ds(rows, c), :].astype(F32)
        return carry

    lax.fori_loop(0, rblk // c, stage, 0)

    ri = _iota((c, c), 0)
    ci = _iota((c, c), 1)
    tri_incl = ri >= ci
    tri_strict = ri > ci
    tri = jnp.where(tri_incl, 1.0, 0.0).astype(BF16)
    later = jnp.where(_iota((c, LANES), 0) > _iota((c, LANES), 1), 1.0, 0.0).astype(F32)

    nu = GDN_UNROLL
    heads = range(N_HEADS)
    hcs = [slice(h * hd, (h + 1) * hd) for h in heads]
    chains = [(u, h) for u in range(nu) for h in heads]

    pre = 8

    def conv(win, col0):
        w = xs_ref[win, col0:col0 + hd]
        cw = conv_ref[:, col0:col0 + hd]
        y = w[pre:] * cw[GDN_CONV - 1:GDN_CONV]
        for j in range(1, GDN_CONV):
            y = y + pltpu.roll(w, j, axis=0)[pre:] * cw[GDN_CONV - 1 - j:GDN_CONV - j]
        return y * _sigmoid(y)

    def chunks(n, carry):
        base = n * (nu * c)
        r0 = [pl.multiple_of(base + u * c, c) for u in range(nu)]
        rows = [pl.ds(r0[u], c) for u in range(nu)]
        win = [pl.ds(pl.multiple_of(r0[u] + (halo - pre), 8), c + pre) for u in range(nu)]
        sm = [sm_ref[rows[u], :] for u in range(nu)]

        q = [conv(win[u], h * hd) for u, h in chains]
        k = [conv(win[u], N_HEADS * hd + h * hd) for u, h in chains]
        v = [conv(win[u], 2 * N_HEADS * hd + h * hd) for u, h in chains]
        q = [x * (lax.rsqrt(jnp.sum(x * x, axis=-1, keepdims=True) + L2_EPS) * (hd ** -0.5))
             for x in q]
        k = [x * lax.rsqrt(jnp.sum(x * x, axis=-1, keepdims=True) + L2_EPS) for x in k]
        bt = [_sigmoid(x) for x in sm]
        gt = [-jnp.exp(hp_ref[0:1, :]) * _softplus(x + hp_ref[1:2, :]) for x in sm]
        beta = [jnp.broadcast_to(bt[u][:, SM_DB + h:SM_DB + h + 1], (c, hd)) for u, h in chains]
        g = [jnp.broadcast_to(gt[u][:, SM_DA + h:SM_DA + h + 1], (c, hd))
             for u, h in chains]

        cum = [_dot(tri, jnp.concatenate([x, x * later], axis=1).astype(BF16)) for x in g]
        gam = [x[:, :hd] for x in cum]
        decay = [jnp.where(tri_incl, jnp.exp(x[:, hd:hd + c]), 0.0) for x in cum]
        eg = [jnp.exp(x) for x in gam]
        gl = [x[c - 1:c, :] for x in gam]

        kb = [x.astype(BF16) for x in k]
        kk = [_dot_nt(x, x) for x in kb]
        qk = [(_dot_nt(q[i].astype(BF16), kb[i]) * decay[i]).astype(BF16) for i in range(len(chains))]
        a = [jnp.where(tri_strict, beta[i][:, :c] * kk[i] * decay[i], 0.0) for i in range(len(chains))]
        tinv = _unit_lower_inverse(a, ri, ci)
        sol = [_dot(tinv[i].astype(BF16),
                    jnp.concatenate([v[i] * beta[i], k[i] * (beta[i] * eg[i])], axis=1).astype(BF16))
               for i in range(len(chains))]
        lhs = [jnp.concatenate([sol[i][:, hd:], q[i] * eg[i]], axis=0).astype(BF16)
               for i in range(len(chains))]
        kd = [(k[i] * jnp.exp(gl[i] - gam[i])).astype(BF16) for i in range(len(chains))]
        cd = [jnp.exp(x) for x in gl]

        state = [st_ref[h] for h in heads]
        outs = []
        for u in range(nu):
            idx = [u * N_HEADS + h for h in heads]
            r = [_dot(lhs[i], state[h].astype(BF16)) for h, i in zip(heads, idx)]
            ub = [(sol[i][:, :hd] - r[h][:c]).astype(BF16) for h, i in zip(heads, idx)]
            outs.extend(r[h][c:] + _dot(qk[i], ub[h]) for h, i in zip(heads, idx))
            state = [state[h] * cd[i] + _dot_tn(kd[i], ub[h]) for h, i in zip(heads, idx)]
        for h in heads:
            st_ref[h] = state[h]
        for i, (u, h) in enumerate(chains):
            o = outs[i]
            on = o * lax.rsqrt(jnp.mean(o * o, axis=-1, keepdims=True) + RMS_EPS) * on_ref[...]
            zg = dz_ref[rows[u], hcs[h]].astype(F32)
            o_ref[rows[u], hcs[h]] = (on * (zg * _sigmoid(zg))).astype(BF16)
        return carry

    lax.fori_loop(0, rblk // (nu * c), chunks, 0)


def _gdn_call(proj3, small3, conv_w, hp, onorm, layer, *, rblk, halo=16):
    b, seq, _ = proj3.shape
    w3 = 3 * N_HEADS * HEAD_DIM
    wz = N_HEADS * HEAD_DIM
    per = rblk // halo
    return pl.pallas_call(
        _gdn_kernel,
        out_shape=jax.ShapeDtypeStruct((b, seq, wz), BF16),
        grid=(b, seq // rblk),
        in_specs=[
            pl.BlockSpec((None, rblk, w3), lambda bi, i: (bi, i, OFF_DQKV // w3)),
            pl.BlockSpec((None, halo, w3),
                         lambda bi, i: (bi, jnp.maximum(i * per - 1, 0), OFF_DQKV // w3)),
            pl.BlockSpec((None, rblk, wz), lambda bi, i: (bi, i, OFF_DZ // wz)),
            pl.BlockSpec((None, rblk, LANES), lambda bi, i: (bi, i, 0)),
            pl.BlockSpec((None, GDN_CONV, w3), lambda bi, i: (layer, 0, 0)),
            pl.BlockSpec((None, 2, LANES), lambda bi, i: (layer, 0, 0)),
            pl.BlockSpec((None, 1, HEAD_DIM), lambda bi, i: (layer, 0, 0)),
        ],
        out_specs=pl.BlockSpec((None, rblk, wz), lambda bi, i: (bi, i, 0)),
        scratch_shapes=[pltpu.VMEM((rblk + halo, w3), F32),
                        pltpu.VMEM((N_HEADS, HEAD_DIM, HEAD_DIM), F32)],
        compiler_params=pltpu.CompilerParams(
            dimension_semantics=("arbitrary", "arbitrary"), vmem_limit_bytes=VMEM_LIMIT),
        name="gdn",
    )(proj3, proj3, proj3, small3, conv_w, hp, onorm)


def _gla_kernel(q_ref, k_ref, v_ref, z_ref, sm_ref, w2_ref, b_ref, on_ref, o_ref, st_ref):
    rblk = q_ref.shape[0]
    hd = HEAD_DIM
    vd = GLA_VD
    c = CHUNK

    @pl.when(pl.program_id(1) == 0)
    def _():
        st_ref[...] = jnp.zeros_like(st_ref)

    tri_incl = _iota((c, c), 0) >= _iota((c, c), 1)
    tri = jnp.where(tri_incl, 1.0, 0.0).astype(BF16)
    nu = GLA_UNROLL
    heads = range(N_HEADS)
    hcs = [slice(h * hd, (h + 1) * hd) for h in heads]
    vcs = [slice(h * vd, (h + 1) * vd) for h in heads]
    chains = [(u, h) for u in range(nu) for h in heads]
    nc = len(chains)

    def chunks(n, carry):
        base = n * (nu * c)
        rows = [pl.ds(pl.multiple_of(base + u * c, c), c) for u in range(nu)]
        la = []
        for u in range(nu):
            x = _dot(sm_ref[rows[u], :].astype(BF16), w2_ref[...]) + b_ref[...]
            la.append(-_softplus(-x) * (1.0 / GLA_TAU))
        gc = [_dot(tri, la[u][:, hcs[h]].astype(BF16)) for u, h in chains]
        gl = [x[c - 1:c, :] for x in gc]
        q = [q_ref[rows[u], hcs[h]].astype(F32) * (hd ** -0.5) for u, h in chains]
        k = [k_ref[rows[u], hcs[h]].astype(F32) for u, h in chains]
        v = [v_ref[rows[u], vcs[h]] for u, h in chains]
        qd = [(q[i] * jnp.exp(gc[i])).astype(BF16) for i in range(nc)]
        att = []
        for i in range(nc):
            parts = []
            for j in range(c // SUB):
                rs = slice(j * SUB, (j + 1) * SUB)
                ref_g = gc[i][j * SUB:j * SUB + 1, :]
                qi = (q[i][rs] * jnp.exp(gc[i][rs] - ref_g)).astype(BF16)
                ki = (k[i] * jnp.exp(jnp.minimum(ref_g - gc[i], 60.0))).astype(BF16)
                parts.append(_dot_nt(qi, ki))
            att.append(jnp.where(tri_incl, jnp.concatenate(parts, axis=0), 0.0).astype(BF16))
        intra = [_dot(att[i], v[i]) for i in range(nc)]
        grow = [_dot_tn(v[i], (k[i] * jnp.exp(gl[i] - gc[i])).astype(BF16)) for i in range(nc)]
        keep = [jnp.exp(x) for x in gl]

        state = [st_ref[h] for h in heads]
        outs = []
        for u in range(nu):
            idx = [u * N_HEADS + h for h in heads]
            outs.extend(intra[i] + _dot_nt(qd[i], state[h].astype(BF16)) for h, i in zip(heads, idx))
            state = [state[h] * keep[i] + grow[i] for h, i in zip(heads, idx)]
        for h in heads:
            st_ref[h] = state[h]
        for i, (u, h) in enumerate(chains):
            o = outs[i]
            on = o * lax.rsqrt(jnp.mean(o * o, axis=-1, keepdims=True) + RMS_EPS) * on_ref[...]
            zg = z_ref[rows[u], vcs[h]].astype(F32)
            o_ref[rows[u], vcs[h]] = (on * (zg * _sigmoid(zg))).astype(BF16)
        return carry

    lax.fori_loop(0, rblk // (nu * c), chunks, 0)


def _gla_call(proj3, small3, w2p, gla_b, onorm, layer, *, rblk):
    b, seq, _ = proj3.shape
    wk = N_HEADS * HEAD_DIM
    wv = N_HEADS * GLA_VD
    return pl.pallas_call(
        _gla_kernel,
        out_shape=jax.ShapeDtypeStruct((b, seq, wv), BF16),
        grid=(b, seq // rblk),
        in_specs=[
            pl.BlockSpec((None, rblk, wk), lambda bi, i: (bi, i, OFF_LQ // wk)),
            pl.BlockSpec((None, rblk, wk), lambda bi, i: (bi, i, OFF_LK // wk)),
            pl.BlockSpec((None, rblk, wv), lambda bi, i: (bi, i, OFF_LV // wv)),
            pl.BlockSpec((None, rblk, wv), lambda bi, i: (bi, i, OFF_LZ // wv)),
            pl.BlockSpec((None, rblk, LANES), lambda bi, i: (bi, i, 0)),
            pl.BlockSpec((None, LANES, wk), lambda bi, i: (layer, 0, 0)),
            pl.BlockSpec((None, 1, wk), lambda bi, i: (layer, 0, 0)),
            pl.BlockSpec((None, 1, GLA_VD), lambda bi, i: (layer, 0, 0)),
        ],
        out_specs=pl.BlockSpec((None, rblk, wv), lambda bi, i: (bi, i, 0)),
        scratch_shapes=[pltpu.VMEM((N_HEADS, GLA_VD, HEAD_DIM), F32)],
        compiler_params=pltpu.CompilerParams(
            dimension_semantics=("arbitrary", "arbitrary"), vmem_limit_bytes=VMEM_LIMIT),
        name="gla",
    )(proj3, proj3, proj3, proj3, small3, w2p, gla_b, onorm)


def _out_kernel(ysb_ref, ygdn_ref, ygla_ref, mg_ref, mb_ref, psb_ref, pgdn_ref, pgla_ref,
                wout_ref, x_ref, mod_ref, o_ref):
    d = x_ref.shape[1]
    gates = _sigmoid(mg_ref[...].astype(F32) + mb_ref[...])
    merged = gates[:, 0:d] * _dot(ysb_ref[...], psb_ref[...])
    merged = merged + gates[:, d:2 * d] * _dot(ygdn_ref[...], pgdn_ref[...])
    merged = merged + gates[:, 2 * d:3 * d] * _dot(ygla_ref[...], pgla_ref[...])
    o_ref[...] = x_ref[...] + mod_ref[:, 2 * d:3 * d] * _dot(merged.astype(BF16), wout_ref[...])


def _out_call(ysb, ygdn, ygla, proj, merge_b, psb, pgdn, pgla, wout, x2, mod, layer, seq, *, tm):
    t, d = x2.shape
    rows_per_seq = seq // tm
    wsb, wgdn, wgla = ysb.shape[1], ygdn.shape[1], ygla.shape[1]

    def wspec(k):
        return pl.BlockSpec((None, k, d), lambda i: (layer, 0, 0))

    return pl.pallas_call(
        _out_kernel,
        out_shape=jax.ShapeDtypeStruct((t, d), F32),
        grid=(t // tm,),
        in_specs=[
            pl.BlockSpec((tm, wsb), lambda i: (i, 0)),
            pl.BlockSpec((tm, wgdn), lambda i: (i, 0)),
            pl.BlockSpec((tm, wgla), lambda i: (i, 0)),
            pl.BlockSpec((tm, N_BRANCH * d), lambda i: (i, OFF_MG // (N_BRANCH * d))),
            pl.BlockSpec((None, 1, N_BRANCH * d), lambda i: (layer, 0, 0)),
            wspec(wsb), wspec(wgdn), wspec(wgla), wspec(d),
            pl.BlockSpec((tm, d), lambda i: (i, 0)),
            pl.BlockSpec((None, None, 1, 3 * d), lambda i: (layer, i // rows_per_seq, 0, 0)),
        ],
        out_specs=pl.BlockSpec((tm, d), lambda i: (i, 0)),
        compiler_params=pltpu.CompilerParams(
            dimension_semantics=("arbitrary",), vmem_limit_bytes=VMEM_LIMIT),
        name="merge_out",
    )(ysb, ygdn, ygla, proj, merge_b, psb, pgdn, pgla, wout, x2, mod)


def _reorder_w_in(w_in):
    hw = N_HEADS * HEAD_DIM
    sb = w_in[..., 0:4 * hw]
    dqkv = w_in[..., 4 * hw:7 * hw]
    dz = w_in[..., 7 * hw:8 * hw]
    o = 8 * hw
    db = w_in[..., o:o + N_HEADS]
    da = w_in[..., o + N_HEADS:o + 2 * N_HEADS]
    o += 2 * N_HEADS
    lqk = w_in[..., o:o + 2 * hw]
    lvz = w_in[..., o + 2 * hw:o + 2 * hw + 2 * N_HEADS * GLA_VD]
    o += 2 * hw + 2 * N_HEADS * GLA_VD
    lr = w_in[..., o:o + GLA_RANK]
    mg = w_in[..., o + GLA_RANK:]
    main = jnp.concatenate([mg, dqkv, dz, sb, lqk, lvz], axis=-1).astype(BF16)
    pad = jnp.zeros(w_in.shape[:-1] + (LANES - GLA_RANK - 2 * N_HEADS,), w_in.dtype)
    small = jnp.concatenate([lr, db, da, pad], axis=-1).astype(BF16)
    return main, small


def kernel(x, c, ada_w, ada_b, norm_g, w_in, sb_qnorm, sb_knorm, gdn_conv, gdn_a_log, gdn_dt_bias,
           gdn_onorm, gla_w2, gla_b, gla_onorm, merge_b, proj_sb, proj_gdn, proj_gla, w_out):
    b, seq, d = x.shape
    n_l = w_in.shape[0]
    t = b * seq

    tl = _tiles(b, seq, d)
    w_main, w_small = _reorder_w_in(w_in)
    assert w_main.shape[-1] == N_MAIN
    mod = _ada_call(c, ada_w, ada_b, tn=tl.ada_cols).reshape(n_l, b, 1, 3 * d)
    hp = jnp.pad(jnp.stack([gdn_a_log, gdn_dt_bias], axis=1),
                 ((0, 0), (0, 0), (SM_DA, LANES - SM_DA - N_HEADS)))
    w2p = jnp.pad(gla_w2, ((0, 0), (SM_LR, LANES - SM_LR - GLA_RANK), (0, 0))).astype(BF16)
    r3 = lambda a: a.reshape(n_l, 1, a.shape[-1])
    psb, pgdn, pgla, wout = (a.astype(BF16) for a in (proj_sb, proj_gdn, proj_gla, w_out))

    x2 = x.reshape(t, d)
    for layer in range(n_l):
        proj, small = _in_call(x2, mod, r3(norm_g), w_main, w_small, layer, seq,
                               tm=tl.in_rows, tn=tl.in_cols)
        proj3 = proj.reshape(b, seq, N_MAIN)
        small3 = small.reshape(b, seq, LANES)
        ysb = _sb_call(proj3, r3(sb_qnorm), r3(sb_knorm), layer, tq=tl.sb_rows, nb=tl.sb_seqs)
        ygdn = _gdn_call(proj3, small3, gdn_conv, hp, r3(gdn_onorm), layer, rblk=tl.lin_rows)
        ygla = _gla_call(proj3, small3, w2p, r3(gla_b), r3(gla_onorm), layer, rblk=tl.lin_rows)
        x2 = _out_call(ysb.reshape(t, -1), ygdn.reshape(t, -1), ygla.reshape(t, -1), proj,
                       r3(merge_b), psb, pgdn, pgla, wout, x2, mod, layer, seq, tm=tl.out_rows)
    return x2.reshape(b, seq, d)
```

```python
import functools
from typing import NamedTuple

import jax
import jax.numpy as jnp
from jax import lax
from jax.experimental import pallas as pl
from jax.experimental.pallas import tpu as pltpu

F32 = jnp.float32
BF16 = jnp.bfloat16

N_HEADS = 4
HEAD_DIM = 128
GLA_VD = 256
GDN_CONV = 4
GLA_RANK = 16
GLA_TAU = 16.0
N_BRANCH = 3
RMS_EPS = 1e-6
L2_EPS = 1e-6
CHUNK = 64
GDN_UNROLL = 8
GLA_UNROLL = 4
SUB = 16
LANES = 128
LOG2E = 1.4426950408889634
SB_DEAD = -160.0
INV_BASE = 8
V7X_VMEM_BYTES = 64 * 1024 * 1024
VMEM_LIMIT = V7X_VMEM_BYTES - 8 * 1024 * 1024

OFF_MG = 0
OFF_DQKV = 3072
OFF_DZ = 4608
OFF_SQ, OFF_SK, OFF_SV, OFF_SZ = 5120, 5632, 6144, 6656
OFF_LQ, OFF_LK, OFF_LV, OFF_LZ = 7168, 7680, 8192, 9216
N_MAIN = 10240
SM_LR, SM_DB, SM_DA = 0, 16, 20


class _Tiles(NamedTuple):
    in_rows: int
    in_cols: int
    out_rows: int
    sb_rows: int
    sb_seqs: int
    lin_rows: int
    ada_cols: int


def _tiles(batch, seq, d_model):
    t = _Tiles(in_rows=min(1024, seq), in_cols=N_MAIN // 4, out_rows=min(512, seq), sb_rows=2 * LANES,
               sb_seqs=min(2, batch), lin_rows=min(1024, seq), ada_cols=3 * d_model // 4)
    assert seq % t.in_rows == 0 and seq % t.out_rows == 0 and seq % t.sb_rows == 0
    assert batch % t.sb_seqs == 0 and seq % t.lin_rows == 0
    assert t.lin_rows % (GDN_UNROLL * CHUNK) == 0 and t.lin_rows % (GLA_UNROLL * CHUNK) == 0
    assert t.in_cols % LANES == 0 and t.ada_cols % LANES == 0
    return t


def _dot(a, b):
    return jnp.dot(a, b, preferred_element_type=F32)


def _dot_nt(a, b):
    return lax.dot_general(a, b, (((1,), (1,)), ((), ())), preferred_element_type=F32)


def _dot_tn(a, b):
    return lax.dot_general(a, b, (((0,), (0,)), ((), ())), preferred_element_type=F32)


def _sigmoid(x):
    return 1.0 / (1.0 + jnp.exp2(x * -LOG2E))


def _softplus(x):
    return jnp.maximum(x, 0.0) + jnp.log(1.0 + jnp.exp(-jnp.abs(x)))


def _iota(shape, axis):
    return lax.broadcasted_iota(jnp.int32, shape, axis)


def _ada_kernel(c_ref, w_ref, b_ref, o_ref):
    c = c_ref[...]
    ca = c * _sigmoid(c)
    o_ref[...] = jnp.dot(ca, w_ref[...], preferred_element_type=F32,
                         precision=lax.Precision.HIGHEST) + b_ref[...]


def _ada_call(c, ada_w, ada_b, *, tn):
    n_l, d, d3 = ada_w.shape
    b = c.shape[0]
    return pl.pallas_call(
        _ada_kernel,
        out_shape=jax.ShapeDtypeStruct((n_l, b, d3), F32),
        grid=(n_l, d3 // tn),
        in_specs=[
            pl.BlockSpec((b, d), lambda l, j: (0, 0)),
            pl.BlockSpec((None, d, tn), lambda l, j: (l, 0, j)),
            pl.BlockSpec((None, 1, tn), lambda l, j: (l, 0, j)),
        ],
        out_specs=pl.BlockSpec((None, b, tn), lambda l, j: (l, 0, j)),
        compiler_params=pltpu.CompilerParams(
            dimension_semantics=("arbitrary", "arbitrary"), vmem_limit_bytes=VMEM_LIMIT),
        name="adaln_mod",
    )(c, ada_w, ada_b.reshape(n_l, 1, d3))


def _in_kernel(x_ref, mod_ref, g_ref, w_ref, ws_ref, out_ref, small_ref, h_ref, *, slab):
    tm, d = x_ref.shape

    @pl.when(pl.program_id(1) == 0)
    def _():
        shift = mod_ref[:, 0:d]
        scale1 = 1.0 + mod_ref[:, d:2 * d]
        gain = g_ref[...]

        def body(s, carry):
            rows = pl.ds(pl.multiple_of(s * slab, slab), slab)
            x = x_ref[rows, :]
            y = x * lax.rsqrt(jnp.mean(x * x, axis=-1, keepdims=True) + RMS_EPS) * gain
            h_ref[rows, :] = (y * scale1 + shift).astype(BF16)
            return carry

        lax.fori_loop(0, tm // slab, body, 0)
        small_ref[...] = _dot(h_ref[...], ws_ref[...])

    out_ref[...] = _dot(h_ref[...], w_ref[...]).astype(BF16)


def _in_call(x2, mod, norm_g, w_main, w_small, layer, seq, *, tm, tn):
    t, d = x2.shape
    n_main = w_main.shape[-1]
    rows_per_seq = seq // tm
    kern = functools.partial(_in_kernel, slab=min(128, tm))
    return pl.pallas_call(
        kern,
        out_shape=(jax.ShapeDtypeStruct((t, n_main), BF16),
                   jax.ShapeDtypeStruct((t, LANES), F32)),
        grid=(t // tm, n_main // tn),
        in_specs=[
            pl.BlockSpec((tm, d), lambda i, j: (i, 0)),
            pl.BlockSpec((None, None, 1, 3 * d), lambda i, j: (layer, i // rows_per_seq, 0, 0)),
            pl.BlockSpec((None, 1, d), lambda i, j: (layer, 0, 0)),
            pl.BlockSpec((None, d, tn), lambda i, j: (layer, 0, j)),
            pl.BlockSpec((None, d, LANES), lambda i, j: (layer, 0, 0)),
        ],
        out_specs=(pl.BlockSpec((tm, tn), lambda i, j: (i, j)),
                   pl.BlockSpec((tm, LANES), lambda i, j: (i, 0))),
        scratch_shapes=[pltpu.VMEM((tm, d), BF16)],
        compiler_params=pltpu.CompilerParams(
            dimension_semantics=("arbitrary", "arbitrary"), vmem_limit_bytes=VMEM_LIMIT),
        name="in_proj",
    )(x2, mod, norm_g, w_main, w_small)


def _sb_kernel(q_ref, k_ref, v_ref, z_ref, qw_ref, kw_ref, o_ref, kn_ref, qn_ref, acc_ref, run_ref,
               *, tq):
    nb, seq = q_ref.shape[0], q_ref.shape[1]
    hd = HEAD_DIM
    scale = hd ** -0.5
    heads = range(N_HEADS)
    hcs = [slice(h * hd, (h + 1) * hd) for h in heads]
    chains = [(b, h, b * N_HEADS + h) for b in range(nb) for h in heads]

    def rms(x, w):
        return x * lax.rsqrt(jnp.mean(x * x, axis=-1, keepdims=True) + RMS_EPS) * w

    def knorm(s, carry):
        rows = pl.ds(pl.multiple_of(s * tq, tq), tq)
        for b, h, _ in chains:
            kn_ref[b, rows, hcs[h]] = rms(k_ref[b, rows, hcs[h]].astype(F32), kw_ref[...]).astype(BF16)
        return carry

    lax.fori_loop(0, seq // tq, knorm, 0)

    later = jnp.where(_iota((tq, tq), 0) > _iota((tq, tq), 1), 1.0, 0.0).astype(BF16)
    half = tq // 2
    mask_top = _iota((half, half), 1) < _iota((half, half), 0)
    mask_bot = _iota((half, tq), 1) < half + _iota((half, tq), 0)

    def run_jobs(jobs, first):
        z = [_dot_nt(qn_ref[s, rs], kn_ref[b, keys, hcs[h]]) for (b, h, s), rs, keys, nk, m in jobs]
        t = [jnp.log2(1.0 + jnp.exp2(-jnp.abs(x))) for x in z]
        lb = [jnp.minimum(x, 0.0) - y for x, y in zip(z, t)]
        lg = [x - y for x, y in zip(lb, z)]
        lg = [x if job[4] is None else jnp.where(job[4], x, 0.0) for x, job in zip(lg, jobs)]
        tail = [_dot(x.astype(BF16), later[:job[3], :job[3]]) for x, job in zip(lg, jobs)]
        ws = []
        for ((b, h, s), rs, keys, nk, m), lbi, lgi, ti in zip(jobs, lb, lg, tail):
            tot = jnp.sum(lgi, axis=1, keepdims=True)
            if first:
                w = jnp.exp2(lbi + ti)
                run_ref[s, rs] = tot
            else:
                run = run_ref[s, rs]
                w = jnp.exp2(lbi + ti + run)
                run_ref[s, rs] = run + tot
            if m is not None:
                w = jnp.where(m, w, 0.0)
            ws.append(w.astype(BF16))
        for ((b, h, s), rs, keys, nk, m), w in zip(jobs, ws):
            pv = _dot(w, v_ref[b, keys, hcs[h]])
            if first:
                acc_ref[s, rs] = pv
            else:
                acc_ref[s, rs] += pv

    def q_block(i, carry):
        r0 = pl.multiple_of(i * tq, tq)
        rows = pl.ds(r0, tq)
        for b, h, s in chains:
            qn_ref[s] = (rms(q_ref[b, rows, hcs[h]].astype(F32), qw_ref[...]) * (scale * LOG2E)).astype(BF16)
        run_jobs([(ch, slice(0, half), pl.ds(r0, half), half, mask_top) for ch in chains]
                 + [(ch, slice(half, tq), pl.ds(r0, tq), tq, mask_bot) for ch in chains], True)

        def alive():
            return jnp.max(run_ref[...]) > SB_DEAD

        def below(carry):
            n, _ = carry
            keys = pl.ds(pl.multiple_of((i - 1 - n) * tq, tq), tq)
            run_jobs([(ch, slice(0, tq), keys, tq, None) for ch in chains], False)
            return n + 1, alive()

        lax.while_loop(lambda carry: (carry[0] < i) & carry[1], below, (0, alive()))
        for b, h, s in chains:
            zg = z_ref[b, rows, hcs[h]].astype(F32)
            o_ref[b, rows, hcs[h]] = (acc_ref[s] * (zg * _sigmoid(zg))).astype(BF16)
        return carry

    lax.fori_loop(0, seq // tq, q_block, 0)


def _sb_call(proj3, qw, kw, layer, *, tq, nb):
    b, seq, _ = proj3.shape
    hd = HEAD_DIM
    w = N_HEADS * hd

    def col(off):
        return pl.BlockSpec((nb, seq, w), lambda bi: (bi, 0, off // w))

    wspec = pl.BlockSpec((None, 1, hd), lambda bi: (layer, 0, 0))
    return pl.pallas_call(
        functools.partial(_sb_kernel, tq=tq),
        out_shape=jax.ShapeDtypeStruct((b, seq, w), BF16),
        grid=(b // nb,),
        in_specs=[col(OFF_SQ), col(OFF_SK), col(OFF_SV), col(OFF_SZ), wspec, wspec],
        out_specs=pl.BlockSpec((nb, seq, w), lambda bi: (bi, 0, 0)),
        scratch_shapes=[pltpu.VMEM((nb, seq, w), BF16),
                        pltpu.VMEM((nb * N_HEADS, tq, hd), BF16),
                        pltpu.VMEM((nb * N_HEADS, tq, hd), F32),
                        pltpu.VMEM((nb * N_HEADS, tq, 1), F32)],
        compiler_params=pltpu.CompilerParams(
            dimension_semantics=("arbitrary",), vmem_limit_bytes=VMEM_LIMIT),
        name="sb_attn",
    )(proj3, proj3, proj3, proj3, qw, kw)


def _unit_lower_inverse(a_list, ri, ci):
    size = ri.shape[0]
    same = lambda w: (ri // w) == (ci // w)
    eye = jnp.where(ri == ci, 1.0, 0.0).astype(F32)
    d = [jnp.where(same(INV_BASE), a, 0.0) for a in a_list]
    x = [eye - y for y in d]
    p = d
    w = 2
    while w < INV_BASE:
        pb = [y.astype(BF16) for y in p]
        p = [_dot(y, y) for y in pb]
        x = [y + _dot(y.astype(BF16), z.astype(BF16)) for y, z in zip(x, p)]
        w *= 2
    w = INV_BASE
    while w < size:
        below = same(2 * w) & jnp.logical_not(same(w))
        xb = [y.astype(BF16) for y in x]
        lx = [_dot(jnp.where(below, a, 0.0).astype(BF16), y) for a, y in zip(a_list, xb)]
        x = [y - _dot(yb, z.astype(BF16)) for y, yb, z in zip(x, xb, lx)]
        w *= 2
    return x


def _gdn_kernel(qkv_ref, prev_ref, dz_ref, sm_ref, conv_ref, hp_ref, on_ref, o_ref,
                xs_ref, st_ref):
    rblk = qkv_ref.shape[0]
    hd = HEAD_DIM
    c = CHUNK
    halo = prev_ref.shape[0]
    blk = pl.program_id(1)

    @pl.when(blk == 0)
    def _():
        st_ref[...] = jnp.zeros_like(st_ref)
        xs_ref[0:halo, :] = jnp.zeros((halo, xs_ref.shape[1]), F32)

    @pl.when(blk > 0)
    def _():
        xs_ref[0:halo, :] = prev_ref[...].astype(F32)

    def stage(s, carry):
        rows = pl.multiple_of(s * c, c)
        xs_ref[pl.ds(halo + rows, c), :] = qkv_ref[pl.ds(rows, c), :].astype(F32)
        return carry

    lax.fori_loop(0, rblk // c, stage, 0)

    ri = _iota((c, c), 0)
    ci = _iota((c, c), 1)
    tri_incl = ri >= ci
    tri_strict = ri > ci
    tri = jnp.where(tri_incl, 1.0, 0.0).astype(BF16)
    later = jnp.where(_iota((c, LANES), 0) > _iota((c, LANES), 1), 1.0, 0.0).astype(F32)

    nu = GDN_UNROLL
    heads = range(N_HEADS)
    hcs = [slice(h * hd, (h + 1) * hd) for h in heads]
    chains = [(u, h) for u in range(nu) for h in heads]

    pre = 8

    def conv(win, col0):
        w = xs_ref[win, col0:col0 + hd]
        cw = conv_ref[:, col0:col0 + hd]
        y = w[pre:] * cw[GDN_CONV - 1:GDN_CONV]
        for j in range(1, GDN_CONV):
            y = y + pltpu.roll(w, j, axis=0)[pre:] * cw[GDN_CONV - 1 - j:GDN_CONV - j]
        return y * _sigmoid(y)

    def chunks(n, carry):
        base = n * (nu * c)
        r0 = [pl.multiple_of(base + u * c, c) for u in range(nu)]
        rows = [pl.ds(r0[u], c) for u in range(nu)]
        win = [pl.ds(pl.multiple_of(r0[u] + (halo - pre), 8), c + pre) for u in range(nu)]
        sm = [sm_ref[rows[u], :] for u in range(nu)]

        q = [conv(win[u], h * hd) for u, h in chains]
        k = [conv(win[u], N_HEADS * hd + h * hd) for u, h in chains]
        v = [conv(win[u], 2 * N_HEADS * hd + h * hd) for u, h in chains]
        q = [x * (lax.rsqrt(jnp.sum(x * x, axis=-1, keepdims=True) + L2_EPS) * (hd ** -0.5))
             for x in q]
        k = [x * lax.rsqrt(jnp.sum(x * x, axis=-1, keepdims=True) + L2_EPS) for x in k]
        bt = [_sigmoid(x) for x in sm]
        gt = [-jnp.exp(hp_ref[0:1, :]) * _softplus(x + hp_ref[1:2, :]) for x in sm]
        beta = [jnp.broadcast_to(bt[u][:, SM_DB + h:SM_DB + h + 1], (c, hd)) for u, h in chains]
        g = [jnp.broadcast_to(gt[u][:, SM_DA + h:SM_DA + h + 1], (c, hd))
             for u, h in chains]

        cum = [_dot(tri, jnp.concatenate([x, x * later], axis=1).astype(BF16)) for x in g]
        gam = [x[:, :hd] for x in cum]
        decay = [jnp.where(tri_incl, jnp.exp(x[:, hd:hd + c]), 0.0) for x in cum]
        eg = [jnp.exp(x) for x in gam]
        gl = [x[c - 1:c, :] for x in gam]

        kb = [x.astype(BF16) for x in k]
        kk = [_dot_nt(x, x) for x in kb]
        qk = [(_dot_nt(q[i].astype(BF16), kb[i]) * decay[i]).astype(BF16) for i in range(len(chains))]
        a = [jnp.where(tri_strict, beta[i][:, :c] * kk[i] * decay[i], 0.0) for i in range(len(chains))]
        tinv = _unit_lower_inverse(a, ri, ci)
        sol = [_dot(tinv[i].astype(BF16),
                    jnp.concatenate([v[i] * beta[i], k[i] * (beta[i] * eg[i])], axis=1).astype(BF16))
               for i in range(len(chains))]
        lhs = [jnp.concatenate([sol[i][:, hd:], q[i] * eg[i]], axis=0).astype(BF16)
               for i in range(len(chains))]
        kd = [(k[i] * jnp.exp(gl[i] - gam[i])).astype(BF16) for i in range(len(chains))]
        cd = [jnp.exp(x) for x in gl]

        state = [st_ref[h] for h in heads]
        outs = []
        for u in range(nu):
            idx = [u * N_HEADS + h for h in heads]
            r = [_dot(lhs[i], state[h].astype(BF16)) for h, i in zip(heads, idx)]
            ub = [(sol[i][:, :hd] - r[h][:c]).astype(BF16) for h, i in zip(heads, idx)]
            outs.extend(r[h][c:] + _dot(qk[i], ub[h]) for h, i in zip(heads, idx))
            state = [state[h] * cd[i] + _dot_tn(kd[i], ub[h]) for h, i in zip(heads, idx)]
        for h in heads:
            st_ref[h] = state[h]
        for i, (u, h) in enumerate(chains):
            o = outs[i]
            on = o * lax.rsqrt(jnp.mean(o * o, axis=-1, keepdims=True) + RMS_EPS) * on_ref[...]
            zg = dz_ref[rows[u], hcs[h]].astype(F32)
            o_ref[rows[u], hcs[h]] = (on * (zg * _sigmoid(zg))).astype(BF16)
        return carry

    lax.fori_loop(0, rblk // (nu * c), chunks, 0)


def _gdn_call(proj3, small3, conv_w, hp, onorm, layer, *, rblk, halo=16):
    b, seq, _ = proj3.shape
    w3 = 3 * N_HEADS * HEAD_DIM
    wz = N_HEADS * HEAD_DIM
    per = rblk // halo
    return pl.pallas_call(
        _gdn_kernel,
        out_shape=jax.ShapeDtypeStruct((b, seq, wz), BF16),
        grid=(b, seq // rblk),
        in_specs=[
            pl.BlockSpec((None, rblk, w3), lambda bi, i: (bi, i, OFF_DQKV // w3)),
            pl.BlockSpec((None, halo, w3),
                         lambda bi, i: (bi, jnp.maximum(i * per - 1, 0), OFF_DQKV // w3)),
            pl.BlockSpec((None, rblk, wz), lambda bi, i: (bi, i, OFF_DZ // wz)),
            pl.BlockSpec((None, rblk, LANES), lambda bi, i: (bi, i, 0)),
            pl.BlockSpec((None, GDN_CONV, w3), lambda bi, i: (layer, 0, 0)),
            pl.BlockSpec((None, 2, LANES), lambda bi, i: (layer, 0, 0)),
            pl.BlockSpec((None, 1, HEAD_DIM), lambda bi, i: (layer, 0, 0)),
        ],
        out_specs=pl.BlockSpec((None, rblk, wz), lambda bi, i: (bi, i, 0)),
        scratch_shapes=[pltpu.VMEM((rblk + halo, w3), F32),
                        pltpu.VMEM((N_HEADS, HEAD_DIM, HEAD_DIM), F32)],
        compiler_params=pltpu.CompilerParams(
            dimension_semantics=("arbitrary", "arbitrary"), vmem_limit_bytes=VMEM_LIMIT),
        name="gdn",
    )(proj3, proj3, proj3, small3, conv_w, hp, onorm)


def _gla_kernel(q_ref, k_ref, v_ref, z_ref, sm_ref, w2_ref, b_ref, on_ref, o_ref, st_ref):
    rblk = q_ref.shape[0]
    hd = HEAD_DIM
    vd = GLA_VD
    c = CHUNK

    @pl.when(pl.program_id(1) == 0)
    def _():
        st_ref[...] = jnp.zeros_like(st_ref)

    tri_incl = _iota((c, c), 0) >= _iota((c, c), 1)
    tri = jnp.where(tri_incl, 1.0, 0.0).astype(BF16)
    nu = GLA_UNROLL
    heads = range(N_HEADS)
    hcs = [slice(h * hd, (h + 1) * hd) for h in heads]
    vcs = [slice(h * vd, (h + 1) * vd) for h in heads]
    chains = [(u, h) for u in range(nu) for h in heads]
    nc = len(chains)

    def chunks(n, carry):
        base = n * (nu * c)
        rows = [pl.ds(pl.multiple_of(base + u * c, c), c) for u in range(nu)]
        la = []
        for u in range(nu):
            x = _dot(sm_ref[rows[u], :].astype(BF16), w2_ref[...]) + b_ref[...]
            la.append(-_softplus(-x) * (1.0 / GLA_TAU))
        gc = [_dot(tri, la[u][:, hcs[h]].astype(BF16)) for u, h in chains]
        gl = [x[c - 1:c, :] for x in gc]
        q = [q_ref[rows[u], hcs[h]].astype(F32) * (hd ** -0.5) for u, h in chains]
        k = [k_ref[rows[u], hcs[h]].astype(F32) for u, h in chains]
        v = [v_ref[rows[u], vcs[h]] for u, h in chains]
        qd = [(q[i] * jnp.exp(gc[i])).astype(BF16) for i in range(nc)]
        att = []
        for i in range(nc):
            parts = []
            for j in range(c // SUB):
                rs = slice(j * SUB, (j + 1) * SUB)
                ref_g = gc[i][j * SUB:j * SUB + 1, :]
                qi = (q[i][rs] * jnp.exp(gc[i][rs] - ref_g)).astype(BF16)
                ki = (k[i] * jnp.exp(jnp.minimum(ref_g - gc[i], 60.0))).astype(BF16)
                parts.append(_dot_nt(qi, ki))
            att.append(jnp.where(tri_incl, jnp.concatenate(parts, axis=0), 0.0).astype(BF16))
        intra = [_dot(att[i], v[i]) for i in range(nc)]
        grow = [_dot_tn(v[i], (k[i] * jnp.exp(gl[i] - gc[i])).astype(BF16)) for i in range(nc)]
        keep = [jnp.exp(x) for x in gl]

        state = [st_ref[h] for h in heads]
        outs = []
        for u in range(nu):
            idx = [u * N_HEADS + h for h in heads]
            outs.extend(intra[i] + _dot_nt(qd[i], state[h].astype(BF16)) for h, i in zip(heads, idx))
            state = [state[h] * keep[i] + grow[i] for h, i in zip(heads, idx)]
        for h in heads:
            st_ref[h] = state[h]
        for i, (u, h) in enumerate(chains):
            o = outs[i]
            on = o * lax.rsqrt(jnp.mean(o * o, axis=-1, keepdims=True) + RMS_EPS) * on_ref[...]
            zg = z_ref[rows[u], vcs[h]].astype(F32)
            o_ref[rows[u], vcs[h]] = (on * (zg * _sigmoid(zg))).astype(BF16)
        return carry

    lax.fori_loop(0, rblk // (nu * c), chunks, 0)


def _gla_call(proj3, small3, w2p, gla_b, onorm, layer, *, rblk):
    b, seq, _ = proj3.shape
    wk = N_HEADS * HEAD_DIM
    wv = N_HEADS * GLA_VD
    return pl.pallas_call(
        _gla_kernel,
        out_shape=jax.ShapeDtypeStruct((b, seq, wv), BF16),
        grid=(b, seq // rblk),
        in_specs=[
            pl.BlockSpec((None, rblk, wk), lambda bi, i: (bi, i, OFF_LQ // wk)),
            pl.BlockSpec((None, rblk, wk), lambda bi, i: (bi, i, OFF_LK // wk)),
            pl.BlockSpec((None, rblk, wv), lambda bi, i: (bi, i, OFF_LV // wv)),
            pl.BlockSpec((None, rblk, wv), lambda bi, i: (bi, i, OFF_LZ // wv)),
            pl.BlockSpec((None, rblk, LANES), lambda bi, i: (bi, i, 0)),
            pl.BlockSpec((None, LANES, wk), lambda bi, i: (layer, 0, 0)),
            pl.BlockSpec((None, 1, wk), lambda bi, i: (layer, 0, 0)),
            pl.BlockSpec((None, 1, GLA_VD), lambda bi, i: (layer, 0, 0)),
        ],
        out_specs=pl.BlockSpec((None, rblk, wv), lambda bi, i: (bi, i, 0)),
        scratch_shapes=[pltpu.VMEM((N_HEADS, GLA_VD, HEAD_DIM), F32)],
        compiler_params=pltpu.CompilerParams(
            dimension_semantics=("arbitrary", "arbitrary"), vmem_limit_bytes=VMEM_LIMIT),
        name="gla",
    )(proj3, proj3, proj3, proj3, small3, w2p, gla_b, onorm)


def _lin_kernel(qkv_ref, prev_ref, dz_ref, sm_ref, conv_ref, hp_ref, gdn_on_ref,
                q_ref, k_ref, v_ref, z_ref, sm2_ref, w2_ref, b_ref, gla_on_ref,
                gdn_o_ref, gla_o_ref, xs_ref, gdn_st_ref, gla_st_ref):
    _gdn_kernel(qkv_ref, prev_ref, dz_ref, sm_ref, conv_ref, hp_ref, gdn_on_ref, gdn_o_ref,
                xs_ref, gdn_st_ref)
    _gla_kernel(q_ref, k_ref, v_ref, z_ref, sm2_ref, w2_ref, b_ref, gla_on_ref, gla_o_ref, gla_st_ref)


def _lin_call(proj3, small3, conv_w, hp, gdn_onorm, w2p, gla_b, gla_onorm, layer, *, rblk, halo=16):
    b, seq, _ = proj3.shape
    w3 = 3 * N_HEADS * HEAD_DIM
    wz = N_HEADS * HEAD_DIM
    wk = N_HEADS * HEAD_DIM
    wv = N_HEADS * GLA_VD
    per = rblk // halo
    par = lambda *shape: pl.BlockSpec((None,) + shape, lambda bi, i: (layer, 0, 0))
    return pl.pallas_call(
        _lin_kernel,
        out_shape=(jax.ShapeDtypeStruct((b, seq, wz), BF16), jax.ShapeDtypeStruct((b, seq, wv), BF16)),
        grid=(b, seq // rblk),
        in_specs=[
            pl.BlockSpec((None, rblk, w3), lambda bi, i: (bi, i, OFF_DQKV // w3)),
            pl.BlockSpec((None, halo, w3),
                         lambda bi, i: (bi, jnp.maximum(i * per - 1, 0), OFF_DQKV // w3)),
            pl.BlockSpec((None, rblk, wz), lambda bi, i: (bi, i, OFF_DZ // wz)),
            pl.BlockSpec((None, rblk, LANES), lambda bi, i: (bi, i, 0)),
            par(GDN_CONV, w3), par(2, LANES), par(1, HEAD_DIM),
            pl.BlockSpec((None, rblk, wk), lambda bi, i: (bi, i, OFF_LQ // wk)),
            pl.BlockSpec((None, rblk, wk), lambda bi, i: (bi, i, OFF_LK // wk)),
            pl.BlockSpec((None, rblk, wv), lambda bi, i: (bi, i, OFF_LV // wv)),
            pl.BlockSpec((None, rblk, wv), lambda bi, i: (bi, i, OFF_LZ // wv)),
            pl.BlockSpec((None, rblk, LANES), lambda bi, i: (bi, i, 0)),
            par(LANES, wk), par(1, wk), par(1, GLA_VD),
        ],
        out_specs=(pl.BlockSpec((None, rblk, wz), lambda bi, i: (bi, i, 0)),
                   pl.BlockSpec((None, rblk, wv), lambda bi, i: (bi, i, 0))),
        scratch_shapes=[pltpu.VMEM((rblk + halo, w3), F32),
                        pltpu.VMEM((N_HEADS, HEAD_DIM, HEAD_DIM), F32),
                        pltpu.VMEM((N_HEADS, GLA_VD, HEAD_DIM), F32)],
        compiler_params=pltpu.CompilerParams(
            dimension_semantics=("arbitrary", "arbitrary"), vmem_limit_bytes=VMEM_LIMIT),
        name="lin_mixers",
    )(proj3, proj3, proj3, small3, conv_w, hp, gdn_onorm, proj3, proj3, proj3, proj3, small3, w2p, gla_b,
      gla_onorm)


def _out_kernel(ysb_ref, ygdn_ref, ygla_ref, mg_ref, mb_ref, psb_ref, pgdn_ref, pgla_ref,
                wout_ref, x_ref, mod_ref, o_ref):
    d = x_ref.shape[1]
    gates = _sigmoid(mg_ref[...].astype(F32) + mb_ref[...])
    merged = gates[:, 0:d] * _dot(ysb_ref[...], psb_ref[...])
    merged = merged + gates[:, d:2 * d] * _dot(ygdn_ref[...], pgdn_ref[...])
    merged = merged + gates[:, 2 * d:3 * d] * _dot(ygla_ref[...], pgla_ref[...])
    o_ref[...] = x_ref[...] + mod_ref[:, 2 * d:3 * d] * _dot(merged.astype(BF16), wout_ref[...])


def _out_call(ysb, ygdn, ygla, proj, merge_b, psb, pgdn, pgla, wout, x2, mod, layer, seq, *, tm):
    t, d = x2.shape
    rows_per_seq = seq // tm
    wsb, wgdn, wgla = ysb.shape[1], ygdn.shape[1], ygla.shape[1]

    def wspec(k):
        return pl.BlockSpec((None, k, d), lambda i: (layer, 0, 0))

    return pl.pallas_call(
        _out_kernel,
        out_shape=jax.ShapeDtypeStruct((t, d), F32),
        grid=(t // tm,),
        in_specs=[
            pl.BlockSpec((tm, wsb), lambda i: (i, 0)),
            pl.BlockSpec((tm, wgdn), lambda i: (i, 0)),
            pl.BlockSpec((tm, wgla), lambda i: (i, 0)),
            pl.BlockSpec((tm, N_BRANCH * d), lambda i: (i, OFF_MG // (N_BRANCH * d))),
            pl.BlockSpec((None, 1, N_BRANCH * d), lambda i: (layer, 0, 0)),
            wspec(wsb), wspec(wgdn), wspec(wgla), wspec(d),
            pl.BlockSpec((tm, d), lambda i: (i, 0)),
            pl.BlockSpec((None, None, 1, 3 * d), lambda i: (layer, i // rows_per_seq, 0, 0)),
        ],
        out_specs=pl.BlockSpec((tm, d), lambda i: (i, 0)),
        compiler_params=pltpu.CompilerParams(
            dimension_semantics=("arbitrary",), vmem_limit_bytes=VMEM_LIMIT),
        name="merge_out",
    )(ysb, ygdn, ygla, proj, merge_b, psb, pgdn, pgla, wout, x2, mod)


def _reorder_w_in(w_in):
    hw = N_HEADS * HEAD_DIM
    sb = w_in[..., 0:4 * hw]
    dqkv = w_in[..., 4 * hw:7 * hw]
    dz = w_in[..., 7 * hw:8 * hw]
    o = 8 * hw
    db = w_in[..., o:o + N_HEADS]
    da = w_in[..., o + N_HEADS:o + 2 * N_HEADS]
    o += 2 * N_HEADS
    lqk = w_in[..., o:o + 2 * hw]
    lvz = w_in[..., o + 2 * hw:o + 2 * hw + 2 * N_HEADS * GLA_VD]
    o += 2 * hw + 2 * N_HEADS * GLA_VD
    lr = w_in[..., o:o + GLA_RANK]
    mg = w_in[..., o + GLA_RANK:]
    main = jnp.concatenate([mg, dqkv, dz, sb, lqk, lvz], axis=-1).astype(BF16)
    pad = jnp.zeros(w_in.shape[:-1] + (LANES - GLA_RANK - 2 * N_HEADS,), w_in.dtype)
    small = jnp.concatenate([lr, db, da, pad], axis=-1).astype(BF16)
    return main, small


def kernel(x, c, ada_w, ada_b, norm_g, w_in, sb_qnorm, sb_knorm, gdn_conv, gdn_a_log, gdn_dt_bias,
           gdn_onorm, gla_w2, gla_b, gla_onorm, merge_b, proj_sb, proj_gdn, proj_gla, w_out):
    b, seq, d = x.shape
    n_l = w_in.shape[0]
    t = b * seq

    tl = _tiles(b, seq, d)
    w_main, w_small = _reorder_w_in(w_in)
    assert w_main.shape[-1] == N_MAIN
    mod = _ada_call(c, ada_w, ada_b, tn=tl.ada_cols).reshape(n_l, b, 1, 3 * d)
    hp = jnp.pad(jnp.stack([gdn_a_log, gdn_dt_bias], axis=1),
                 ((0, 0), (0, 0), (SM_DA, LANES - SM_DA - N_HEADS)))
    w2p = jnp.pad(gla_w2, ((0, 0), (SM_LR, LANES - SM_LR - GLA_RANK), (0, 0))).astype(BF16)
    r3 = lambda a: a.reshape(n_l, 1, a.shape[-1])
    psb, pgdn, pgla, wout = (a.astype(BF16) for a in (proj_sb, proj_gdn, proj_gla, w_out))

    x2 = x.reshape(t, d)
    for layer in range(n_l):
        proj, small = _in_call(x2, mod, r3(norm_g), w_main, w_small, layer, seq,
                               tm=tl.in_rows, tn=tl.in_cols)
        proj3 = proj.reshape(b, seq, N_MAIN)
        small3 = small.reshape(b, seq, LANES)
        ysb = _sb_call(proj3, r3(sb_qnorm), r3(sb_knorm), layer, tq=tl.sb_rows, nb=tl.sb_seqs)
        ygdn, ygla = _lin_call(proj3, small3, gdn_conv, hp, r3(gdn_onorm), w2p, r3(gla_b), r3(gla_onorm),
                               layer, rblk=tl.lin_rows)
        x2 = _out_call(ysb.reshape(t, -1), ygdn.reshape(t, -1), ygla.reshape(t, -1), proj,
                       r3(merge_b), psb, pgdn, pgla, wout, x2, mod, layer, seq, tm=tl.out_rows)
    return x2.reshape(b, seq, d)
```
